```python
import jax, jax.numpy as jnp
from jax import lax
import numpy as np

D_MODEL = 1024
BATCH = 4
SEQ = 8192
DEPTH = 2

RW_HEADS = 8
RW_HEAD_DIM = 64
RW_WIDTH = RW_HEADS * RW_HEAD_DIM
RW_DECAY_RANK = 64
RW_ICLR_RANK = 64
RW_GATE_RANK = 128
RW_COLS = 3 * RW_WIDTH + RW_DECAY_RANK + RW_ICLR_RANK + RW_GATE_RANK
RW_GN_EPS = 64e-5
RET_HEADS = 4
RET_QK_DIM = 64
RET_V_DIM = 128
RET_QK_WIDTH = RET_HEADS * RET_QK_DIM
RET_V_WIDTH = RET_HEADS * RET_V_DIM
RET_COLS = 2 * RET_QK_WIDTH + 2 * RET_V_WIDTH
RET_CHUNK = 128
ROPE_BASE = 10000.0
SG_GROUPS = 4
SG_CHUNK = 128
SG_WIDTH = 512
SG_GROUP_DIM = SG_WIDTH // SG_GROUPS
SG_COLS = 2 * SG_WIDTH
N_BRANCH = 3
BRANCH_WIDTH = 512
GATE_COLS = N_BRANCH * D_MODEL
IN_COLS = RW_COLS + RET_COLS + SG_COLS + GATE_COLS
D_FF = 2816
CONV_WIDTH = 3
EPS = 1e-6

kernel_name = "hybrid_rwkv7_retention_sgu_gated_merge"


def rmsnorm(x, g):
    xf = x.astype(jnp.float32)
    y = xf * lax.rsqrt(jnp.mean(xf * xf, axis=-1, keepdims=True) + EPS) * g.astype(jnp.float32)
    return y.astype(x.dtype)


def rwkv7_scan(r, w, k, v, kk, a):
    B, S, H, N = r.shape

    def step(state, inp):
        r_t, w_t, k_t, v_t, kk_t, a_t = inp
        sa = jnp.einsum('bhvk,bhk->bhv', state, -kk_t)
        state = (state * w_t[:, :, None, :]
                 + sa[..., None] * (kk_t * a_t)[:, :, None, :]
                 + v_t[..., None] * k_t[:, :, None, :])
        y = jnp.einsum('bhvk,bhk->bhv', state, r_t)
        return state, y

    xs = tuple(jnp.moveaxis(t.astype(jnp.float32), 1, 0) for t in (r, w, k, v, kk, a))
    s0 = jnp.zeros((B, H, N, N), jnp.float32)
    _, ys = lax.scan(step, s0, xs)
    return jnp.moveaxis(ys, 0, 1)


def rwkv7_branch(p, mu, w0, w_up, a0, a_up, g_up, k_k, k_a, r_k, lnx_g, lnx_b):
    B, S, _ = p.shape
    p_prev = jnp.pad(p, ((0, 0), (1, 0), (0, 0)))[:, :S]
    p = p + mu * (p_prev - p)
    cuts = [RW_WIDTH, 2 * RW_WIDTH, 3 * RW_WIDTH, 3 * RW_WIDTH + RW_DECAY_RANK,
            3 * RW_WIDTH + RW_DECAY_RANK + RW_ICLR_RANK]
    r, k, v, wd, ad, gd = jnp.split(p, cuts, axis=-1)
    w = -jax.nn.softplus(-(w0 + jnp.tanh(wd) @ w_up)) - 0.5
    decay = jnp.exp(-jnp.exp(w.astype(jnp.float32)))
    a = jax.nn.sigmoid(a0 + ad @ a_up)
    g = jax.nn.sigmoid(gd) @ g_up
    kk = (k * k_k).astype(jnp.float32)
    k = k * (1.0 + (a - 1.0) * k_a)

    def heads(t):
        return t.reshape(B, S, RW_HEADS, RW_HEAD_DIM)

    r, k, v, decay, a, kk = map(heads, (r, k, v, decay, a, kk))
    kk = kk / jnp.maximum(jnp.sqrt(jnp.sum(kk * kk, axis=-1, keepdims=True)), 1e-12)
    y = rwkv7_scan(r, decay, k, v, kk, a)
    mean = jnp.mean(y, axis=-1, keepdims=True)
    var = jnp.mean(jnp.square(y - mean), axis=-1, keepdims=True)
    y = (y - mean) * lax.rsqrt(var + RW_GN_EPS) * lnx_g + lnx_b
    y = y + jnp.sum(r * k * r_k, axis=-1, keepdims=True) * v
    return (y.reshape(B, S, RW_WIDTH) * g).astype(p.dtype)


def rotary(t, cos, sin):
    t1, t2 = jnp.split(t, 2, axis=-1)
    c = cos[None, :, None, :]
    s = sin[None, :, None, :]
    return jnp.concatenate([t1 * c - t2 * s, t2 * c + t1 * s], axis=-1)


def retention_branch(p, cos, sin):
    B, S, _ = p.shape
    nC = S // RET_CHUNK
    q, k, v, g = jnp.split(p, [RET_QK_WIDTH, 2 * RET_QK_WIDTH, 2 * RET_QK_WIDTH + RET_V_WIDTH], axis=-1)
    q = rotary(q.reshape(B, S, RET_HEADS, RET_QK_DIM), cos, sin)
    k = rotary(k.reshape(B, S, RET_HEADS, RET_QK_DIM), cos, sin) * (RET_QK_DIM ** -0.5)
    v = v.reshape(B, S, RET_HEADS, RET_V_DIM)

    def chunks(t):
        return t.reshape(B, nC, RET_CHUNK, RET_HEADS, t.shape[-1]).transpose(0, 3, 1, 2, 4)

    qc, kc, vc = chunks(q), chunks(k), chunks(v)
    log_gamma = jnp.log(1.0 - 2.0 ** (-5.0 - jnp.arange(RET_HEADS, dtype=jnp.float32)))
    pos = jnp.arange(RET_CHUNK, dtype=jnp.float32)
    diff = pos[:, None] - pos[None, :]
    decay_in = jnp.where(diff[None] >= 0,
                         jnp.exp(jnp.maximum(diff, 0.0)[None] * log_gamma[:, None, None]), 0.0)
    scores = jnp.einsum('bhnid,bhnjd->bhnij', qc, kc) * decay_in[:, None]
    inner = jnp.einsum('bhnij,bhnje->bhnie', scores, vc)
    k_dec = jnp.exp((RET_CHUNK - 1.0 - pos)[None] * log_gamma[:, None])
    kv = jnp.einsum('bhnjd,bhnje,hj->nbhde', kc, vc, k_dec).astype(jnp.float32)
    chunk_decay = jnp.exp(RET_CHUNK * log_gamma)[None, :, None, None]

    def step(R, kv_n):
        return R * chunk_decay + kv_n, R

    _, R_prev = lax.scan(step, jnp.zeros((B, RET_HEADS, RET_QK_DIM, RET_V_DIM), jnp.float32), kv)
    q_dec = jnp.exp((pos + 1.0)[None] * log_gamma[:, None])
    cross = jnp.einsum('bhnid,nbhde,hi->bhnie', qc, R_prev, q_dec)
    y = (inner + cross).transpose(0, 2, 3, 1, 4).reshape(B, S, RET_HEADS, RET_V_DIM)
    y = y * lax.rsqrt(jnp.mean(y * y, axis=-1, keepdims=True) + EPS)
    return (y.reshape(B, S, RET_V_WIDTH) * jax.nn.silu(g)).astype(p.dtype)


def sgu_branch(p, ln_g, ln_b, w_s, b_s):
    B, S, _ = p.shape
    nC = S // SG_CHUNK
    z = jax.nn.gelu(p, approximate=False)
    u, v = jnp.split(z, 2, axis=-1)
    vf = v.astype(jnp.float32)
    mean = jnp.mean(vf, axis=-1, keepdims=True)
    var = jnp.mean(jnp.square(vf - mean), axis=-1, keepdims=True)
    v = ((vf - mean) * lax.rsqrt(var + EPS) * ln_g + ln_b).astype(p.dtype)
    vc = v.reshape(B, nC, SG_CHUNK, SG_GROUPS, SG_GROUP_DIM)
    causal = jnp.tril(jnp.ones((SG_CHUNK, SG_CHUNK), dtype=bool))
    w = jnp.where(causal[None], w_s, 0.0)
    mixed = jnp.einsum('gij,bnjgd->bnigd', w, vc) + b_s.T[:, :, None]
    return u * mixed.reshape(B, S, SG_WIDTH)


def conv_ffn(h, w_up, conv_w, conv_b, w_down):
    S = h.shape[1]
    u = h @ w_up
    up = jnp.pad(u, ((0, 0), (CONV_WIDTH - 1, 0), (0, 0)))
    c = conv_b
    for j in range(CONV_WIDTH):
        c = c + conv_w[j] * up[:, j:j + S]
    gate, val = jnp.split(c, 2, axis=-1)
    return (jax.nn.silu(gate) * val) @ w_down


def _normal(k, shape, scale):
    return scale * jax.random.normal(k, shape, jnp.float32)


def setup_inputs(seed: int = 0) -> dict:
    key = jax.random.key(seed)
    ks = jax.random.split(key, 32)
    L = DEPTH
    f32 = jnp.float32
    return {
        "x": jax.random.normal(ks[0], (BATCH, SEQ, D_MODEL), f32),
        "norm1_g": 1.0 + _normal(ks[1], (L, D_MODEL), 0.05),
        "w_in": _normal(ks[2], (L, D_MODEL, IN_COLS), D_MODEL ** -0.5),
        "rw_mu": jax.random.uniform(ks[3], (L, RW_COLS), f32),
        "rw_w0": jax.random.uniform(ks[4], (L, RW_WIDTH), f32, -6.0, 1.0),
        "rw_w_up": _normal(ks[5], (L, RW_DECAY_RANK, RW_WIDTH), 0.1 * RW_DECAY_RANK ** -0.5),
        "rw_a0": _normal(ks[6], (L, RW_WIDTH), 0.1),
        "rw_a_up": _normal(ks[7], (L, RW_ICLR_RANK, RW_WIDTH), 0.3 * RW_ICLR_RANK ** -0.5),
        "rw_g_up": _normal(ks[8], (L, RW_GATE_RANK, RW_WIDTH), RW_GATE_RANK ** -0.5),
        "rw_k_k": 0.85 + _normal(ks[9], (L, RW_WIDTH), 0.05),
        "rw_k_a": 1.0 + _normal(ks[10], (L, RW_WIDTH), 0.05),
        "rw_r_k": _normal(ks[11], (L, RW_HEADS, RW_HEAD_DIM), 0.1),
        "rw_lnx_g": 1.0 + _normal(ks[12], (L, RW_HEADS, RW_HEAD_DIM), 0.05),
        "rw_lnx_b": _normal(ks[13], (L, RW_HEADS, RW_HEAD_DIM), 0.02),
        "sg_ln_g": 1.0 + _normal(ks[14], (L, SG_WIDTH), 0.05),
        "sg_ln_b": _normal(ks[15], (L, SG_WIDTH), 0.02),
        "sg_w_s": _normal(ks[16], (L, SG_GROUPS, SG_CHUNK, SG_CHUNK), SG_CHUNK ** -0.5),
        "sg_b": 1.0 + _normal(ks[17], (L, SG_GROUPS, SG_CHUNK), 0.1),
        "w_branch": _normal(ks[18], (L, N_BRANCH, BRANCH_WIDTH, D_MODEL), BRANCH_WIDTH ** -0.5),
        "w_out": _normal(ks[19], (L, D_MODEL, D_MODEL), 0.5 * D_MODEL ** -0.5),
        "norm2_g": 1.0 + _normal(ks[20], (L, D_MODEL), 0.05),
        "ffn_w_up": _normal(ks[21], (L, D_MODEL, 2 * D_FF), D_MODEL ** -0.5),
        "ffn_conv_w": _normal(ks[22], (L, CONV_WIDTH, 2 * D_FF), CONV_WIDTH ** -0.5),
        "ffn_conv_b": _normal(ks[23], (L, 2 * D_FF), 0.02),
        "ffn_w_down": _normal(ks[24], (L, D_FF, D_MODEL), 0.5 * D_FF ** -0.5),
        "final_g": 1.0 + _normal(ks[25], (D_MODEL,), 0.05),
    }


def reference(x, norm1_g, w_in, rw_mu, rw_w0, rw_w_up, rw_a0, rw_a_up, rw_g_up, rw_k_k, rw_k_a,
              rw_r_k, rw_lnx_g, rw_lnx_b, sg_ln_g, sg_ln_b, sg_w_s, sg_b, w_branch, w_out,
              norm2_g, ffn_w_up, ffn_conv_w, ffn_conv_b, ffn_w_down, final_g):
    B, S, _ = x.shape
    inv_freq = 1.0 / (ROPE_BASE ** jnp.linspace(0.0, 1.0, RET_QK_DIM // 2, dtype=jnp.float32))
    ang = jnp.arange(S, dtype=jnp.float32)[:, None] * inv_freq[None, :]
    cos, sin = jnp.cos(ang), jnp.sin(ang)
    in_cuts = [RW_COLS, RW_COLS + RET_COLS, RW_COLS + RET_COLS + SG_COLS]
    for l in range(DEPTH):
        h = rmsnorm(x, norm1_g[l])
        p = h @ w_in[l]
        p_rw, p_ret, p_sg, gate_logits = jnp.split(p, in_cuts, axis=-1)
        y_rw = rwkv7_branch(p_rw, rw_mu[l], rw_w0[l], rw_w_up[l], rw_a0[l], rw_a_up[l], rw_g_up[l],
                            rw_k_k[l], rw_k_a[l], rw_r_k[l], rw_lnx_g[l], rw_lnx_b[l])
        y_ret = retention_branch(p_ret, cos, sin)
        y_sg = sgu_branch(p_sg, sg_ln_g[l], sg_ln_b[l], sg_w_s[l], sg_b[l])
        branches = jnp.stack([y_rw, y_ret, y_sg], axis=2)
        proj = jnp.einsum('bsgc,gcd->bsgd', branches, w_branch[l])
        gates = jax.nn.sigmoid(gate_logits.reshape(B, S, N_BRANCH, D_MODEL))
        x = x + jnp.sum(gates * proj, axis=2) @ w_out[l]
        x = x + conv_ffn(rmsnorm(x, norm2_g[l]), ffn_w_up[l], ffn_conv_w[l], ffn_conv_b[l], ffn_w_down[l])
    return rmsnorm(x, final_g)
```

```python
import functools

import jax
import jax.numpy as jnp
from jax import lax
from jax.experimental import pallas as pl
from jax.experimental.pallas import tpu as pltpu

F32 = jnp.float32
BF16 = jnp.bfloat16

D_MODEL = 1024
RW_HEADS = 8
RW_HEAD_DIM = 64
RW_WIDTH = 512
RW_DECAY_RANK = 64
RW_ICLR_RANK = 64
RW_GATE_RANK = 128
RW_COLS = 1792
RW_GN_EPS = 64e-5
RW_CHUNK = 64
RET_HEADS = 4
RET_QK_DIM = 64
RET_V_DIM = 128
RET_QK_WIDTH = 256
RET_V_WIDTH = 512
RET_COLS = 1536
RET_CHUNK = 128
ROPE_BASE = 10000.0
SG_GROUPS = 4
SG_CHUNK = 128
SG_WIDTH = 512
SG_COLS = 1024
N_BRANCH = 3
GATE_COLS = 3072
D_FF = 2816
CONV_WIDTH = 3
EPS = 1e-6

V7X_LANES = 128
V7X_VMEM_LIMIT_BYTES = 56 * 1024 * 1024


def _dot(a, b):
    return jnp.dot(a.astype(BF16), b.astype(BF16), preferred_element_type=F32)


def _dot_nt(a, b):
    return lax.dot_general(a.astype(BF16), b.astype(BF16), (((1,), (1,)), ((), ())),
                           preferred_element_type=F32)


def _dot_tn(a, b):
    return lax.dot_general(a.astype(BF16), b.astype(BF16), (((0,), (0,)), ((), ())),
                           preferred_element_type=F32)


def _split(a):
    hi = a.astype(BF16)
    lo = (a - hi.astype(F32)).astype(BF16)
    return hi, lo


def _dot_exact_rhs(a, b_bf16):
    hi, lo = _split(a)
    return (jnp.dot(hi, b_bf16, preferred_element_type=F32)
            + jnp.dot(lo, b_bf16, preferred_element_type=F32))


def _dot_exact_lhs(a_bf16, b):
    hi, lo = _split(b)
    return (jnp.dot(a_bf16, hi, preferred_element_type=F32)
            + jnp.dot(a_bf16, lo, preferred_element_type=F32))


def _sigmoid(x):
    return 1.0 / (1.0 + jnp.exp(-x))


def _rmsnorm(x, g):
    return x * lax.rsqrt(jnp.mean(x * x, axis=-1, keepdims=True) + EPS) * g


def _params(sem):
    return pltpu.CompilerParams(dimension_semantics=sem, vmem_limit_bytes=V7X_VMEM_LIMIT_BYTES)


def _const_spec(shape):
    nd = len(shape)
    return pl.BlockSpec(shape, lambda *_: (0,) * nd)


def _inproj_kernel(x_ref, g_ref, w_ref, o_ref, h_ref):
    @pl.when(pl.program_id(1) == 0)
    def _():
        h_ref[...] = _rmsnorm(x_ref[...], g_ref[...]).astype(BF16)

    o_ref[...] = jnp.dot(h_ref[...], w_ref[...], preferred_element_type=F32)


def _inproj(x2, g, w, tm, tn):
    T, D = x2.shape
    N = w.shape[1]
    return pl.pallas_call(
        _inproj_kernel,
        out_shape=jax.ShapeDtypeStruct((T, N), F32),
        grid=(T // tm, N // tn),
        in_specs=[pl.BlockSpec((tm, D), lambda i, j: (i, 0)),
                  pl.BlockSpec((1, D), lambda i, j: (0, 0)),
                  pl.BlockSpec((D, tn), lambda i, j: (0, j))],
        out_specs=pl.BlockSpec((tm, tn), lambda i, j: (i, j)),
        scratch_shapes=[pltpu.VMEM((tm, D), BF16)],
        compiler_params=_params(("parallel", "arbitrary")),
        name="inproj",
    )(x2, g, w)


def _rwkv_kernel(p_ref, mu_ref, w0_ref, a0_ref, wa_up_ref, g_up_ref, kk_ref, ka_ref, rk_ref,
                 lng_ref, lnb_ref, o_ref,
                 carry, hst, r_s, lw_s, k_s, v_s, al_s, be_s, y_s, *, ts):
    C = RW_CHUNK
    W = RW_WIDTH

    @pl.when(pl.program_id(1) == 0)
    def _():
        carry[...] = jnp.zeros_like(carry)
        hst[...] = jnp.zeros_like(hst)

    p = p_ref[...]
    row = lax.broadcasted_iota(jnp.int32, (ts, 1), 0)
    p_prev = jnp.where(row == 0, carry[7:8, :], pltpu.roll(p, 1, axis=0))
    carry[...] = p[ts - 8:ts, :]
    xs = p + mu_ref[...] * (p_prev - p)
    r = xs[:, 0:W]
    k = xs[:, W:2 * W]
    v = xs[:, 2 * W:3 * W]
    wa = xs[:, 3 * W:3 * W + 128]
    gd = xs[:, 3 * W + 128:3 * W + 256]
    lane128 = lax.broadcasted_iota(jnp.int32, (1, 128), 1)
    wa = jnp.where(lane128 < RW_DECAY_RANK, jnp.tanh(wa), wa)
    low = _dot(wa, wa_up_ref[...])
    z = -(w0_ref[...] + low[:, 0:W])
    softplus = jnp.maximum(z, 0.0) + jnp.log(1.0 + jnp.exp(-jnp.abs(z)))
    lw = -jnp.exp(-softplus - 0.5)
    a = _sigmoid(a0_ref[...] + low[:, W:2 * W])
    g = _dot(_sigmoid(gd), g_up_ref[...])

    ri = lax.broadcasted_iota(jnp.int32, (W, W), 0) // RW_HEAD_DIM
    ci = lax.broadcasted_iota(jnp.int32, (W, W), 1) // RW_HEAD_DIM
    headsum = jnp.where(ri == ci, 1.0, 0.0).astype(BF16)

    kk = k * kk_ref[...]
    ss = _dot_exact_rhs(kk * kk, headsum)
    kk = kk / jnp.maximum(jnp.sqrt(ss), 1e-12)
    k2 = k * (1.0 + (a - 1.0) * ka_ref[...])
    r_s[...] = r
    lw_s[...] = lw
    k_s[...] = k2
    v_s[...] = v
    al_s[...] = -kk
    be_s[...] = kk * a

    ti = lax.broadcasted_iota(jnp.int32, (C, C), 0)
    tj = lax.broadcasted_iota(jnp.int32, (C, C), 1)
    tri_incl = jnp.where(ti >= tj, 1.0, 0.0).astype(BF16)
    lower_incl = ti >= tj
    lower_strict = ti > tj
    first_half = lane128 < RW_HEAD_DIM
    bi = lax.broadcasted_iota(jnp.int32, (128, 128), 0) // RW_HEAD_DIM
    bj = lax.broadcasted_iota(jnp.int32, (128, 128), 1) // RW_HEAD_DIM
    same_head = bi == bj

    def chunk_body(c, _):
        sl = pl.ds(pl.multiple_of(c * C, C), C)
        lwc = lw_s[sl, :]
        logp = _dot_exact_lhs(tri_incl, lwc)
        logp_last = logp[C - 1:C, :]
        e_p = jnp.exp(logp)
        e_n = jnp.exp(-logp)
        e_x = jnp.exp(logp - lwc)
        e_l = jnp.exp(logp_last - logp)
        p_end = jnp.exp(logp_last)
        rc = r_s[sl, :]
        kc = k_s[sl, :]
        vc = v_s[sl, :]
        alc = al_s[sl, :]
        bec = be_s[sl, :]
        rb = rc * e_p
        ab = alc * e_x
        bt = bec * e_n
        kt = kc * e_n
        btp = bec * e_l
        ktp = kc * e_l
        for pr in range(RW_HEADS // 2):
            ls = slice(pr * 128, (pr + 1) * 128)
            ar = jnp.concatenate([ab[:, ls], rb[:, ls]], axis=0)
            bk = jnp.concatenate([bt[:, ls], kt[:, ls]], axis=0)
            bkp = jnp.concatenate([btp[:, ls], ktp[:, ls]], axis=0)
            vp = vc[:, ls]
            hs = hst[pr]
            arh = _dot_nt(ar, hs)
            u_pair = None
            o_pair = None
            for hh in range(2):
                msk = first_half if hh == 0 else jnp.logical_not(first_half)
                arm = jnp.where(msk, ar, 0.0)
                smat = _dot_nt(arm, bk)
                s_ab = jnp.where(lower_strict, smat[0:C, 0:C], 0.0)
                s_ak = jnp.where(lower_strict, smat[0:C, C:2 * C], 0.0)
                s_rb = jnp.where(lower_incl, smat[C:2 * C, 0:C], 0.0)
                s_rk = jnp.where(lower_incl, smat[C:2 * C, C:2 * C], 0.0)
                u = arh[0:C, :] + _dot(s_ak, vp)
                pm = s_ab
                u = u + _dot(pm, u)
                for _ in range(5):
                    pm = _dot(pm, pm)
                    u = u + _dot(pm, u)
                srbk = jnp.concatenate([s_rb, s_rk], axis=1)
                uv = jnp.concatenate([u, vp], axis=0)
                o = arh[C:2 * C, :] + _dot(srbk, uv)
                if hh == 0:
                    u_pair, o_pair = u, o
                else:
                    u_pair = jnp.where(first_half, u_pair, u)
                    o_pair = jnp.where(first_half, o_pair, o)
            y_s[sl, ls] = o_pair
            uvp = jnp.concatenate([u_pair, vp], axis=0)
            gmat = _dot_tn(uvp, bkp)
            hst[pr] = hs * p_end[:, ls] + jnp.where(same_head, gmat, 0.0)
        return 0

    lax.fori_loop(0, ts // C, chunk_body, 0)

    y = y_s[...]
    inv_n = 1.0 / RW_HEAD_DIM
    mean = _dot_exact_rhs(y, headsum) * inv_n
    d = y - mean
    var = _dot_exact_rhs(d * d, headsum) * inv_n
    yn = d * lax.rsqrt(var + RW_GN_EPS) * lng_ref[...] + lnb_ref[...]
    bonus = _dot_exact_rhs(r * k2 * rk_ref[...], headsum) * v
    o_ref[...] = (yn + bonus) * g


def _rwkv(p3, mu, w0, a0, wa_up, g_up, k_k, k_a, r_k, lng, lnb, ts):
    B, S, _ = p3.shape
    kern = functools.partial(_rwkv_kernel, ts=ts)
    vec = lambda n: _const_spec((1, n))
    tile = lambda: pltpu.VMEM((ts, RW_WIDTH), F32)
    return pl.pallas_call(
        kern,
        out_shape=jax.ShapeDtypeStruct((B, S, RW_WIDTH), F32),
        grid=(B, S // ts),
        in_specs=[pl.BlockSpec((None, ts, RW_COLS), lambda b, s: (b, s, 0)),
                  vec(RW_COLS), vec(RW_WIDTH), vec(RW_WIDTH),
                  _const_spec((128, 2 * RW_WIDTH)), _const_spec((RW_GATE_RANK, RW_WIDTH)),
                  vec(RW_WIDTH), vec(RW_WIDTH), vec(RW_WIDTH), vec(RW_WIDTH), vec(RW_WIDTH)],
        out_specs=pl.BlockSpec((None, ts, RW_WIDTH), lambda b, s: (b, s, 0)),
        scratch_shapes=[pltpu.VMEM((8, RW_COLS), F32),
                        pltpu.VMEM((RW_HEADS // 2, 128, 128), F32),
                        tile(), tile(), tile(), tile(), tile(), tile(), tile()],
        compiler_params=_params(("parallel", "arbitrary")),
        name="rwkv7",
    )(p3, mu, w0, a0, wa_up, g_up, k_k, k_a, r_k, lng, lnb)


def _ret_kernel(p_ref, cos_ref, sin_ref, din_ref, qd_ref, kd_ref, cd_ref, o_ref, rst, *, ts):
    C = RET_CHUNK

    @pl.when(pl.program_id(1) == 0)
    def _():
        rst[...] = jnp.zeros_like(rst)

    lane = lax.broadcasted_iota(jnp.int32, (1, RET_QK_WIDTH), 1)
    first = (lane % RET_QK_DIM) < (RET_QK_DIM // 2)
    lane128 = lax.broadcasted_iota(jnp.int32, (1, 128), 1)
    first_head = lane128 < RET_QK_DIM

    def rot(t, cs, sn):
        sw = jnp.where(first, pltpu.roll(t, RET_QK_WIDTH - RET_QK_DIM // 2, axis=1),
                       pltpu.roll(t, RET_QK_DIM // 2, axis=1))
        return t * cs + sw * sn

    for c in range(ts // C):
        rows = slice(c * C, (c + 1) * C)
        cs = cos_ref[rows, :]
        sn = sin_ref[rows, :]
        q = rot(p_ref[rows, 0:RET_QK_WIDTH], cs, sn)
        k = rot(p_ref[rows, RET_QK_WIDTH:2 * RET_QK_WIDTH], cs, sn) * (RET_QK_DIM ** -0.5)
        for h in range(RET_HEADS):
            pr = h // 2
            msk = first_head if h % 2 == 0 else jnp.logical_not(first_head)
            qm = jnp.where(msk, q[:, pr * 128:(pr + 1) * 128], 0.0)
            kp = k[:, pr * 128:(pr + 1) * 128]
            vh = p_ref[rows, 2 * RET_QK_WIDTH + h * RET_V_DIM:2 * RET_QK_WIDTH + (h + 1) * RET_V_DIM]
            gh = p_ref[rows, 2 * RET_QK_WIDTH + RET_V_WIDTH + h * RET_V_DIM:
                       2 * RET_QK_WIDTH + RET_V_WIDTH + (h + 1) * RET_V_DIM]
            scores = _dot_nt(qm, kp) * din_ref[h]
            inner = _dot(scores, vh)
            rh = rst[h]
            cross = _dot(qm, rh) * qd_ref[h]
            y = inner + cross
            km = jnp.where(msk, kp, 0.0) * kd_ref[h]
            rst[h] = rh * cd_ref[h] + _dot_tn(km, vh)
            y = y * lax.rsqrt(jnp.mean(y * y, axis=-1, keepdims=True) + EPS)
            o_ref[rows, h * RET_V_DIM:(h + 1) * RET_V_DIM] = y * (gh * _sigmoid(gh))


def _retention(p3, cosf, sinf, din, qd, kd, cd, ts):
    B, S, _ = p3.shape
    kern = functools.partial(_ret_kernel, ts=ts)
    tab = lambda: _const_spec((RET_HEADS, RET_CHUNK, 128))
    return pl.pallas_call(
        kern,
        out_shape=jax.ShapeDtypeStruct((B, S, RET_V_WIDTH), F32),
        grid=(B, S // ts),
        in_specs=[pl.BlockSpec((None, ts, RET_COLS), lambda b, s: (b, s, 0)),
                  pl.BlockSpec((ts, RET_QK_WIDTH), lambda b, s: (s, 0)),
                  pl.BlockSpec((ts, RET_QK_WIDTH), lambda b, s: (s, 0)),
                  tab(), tab(), tab(), tab()],
        out_specs=pl.BlockSpec((None, ts, RET_V_WIDTH), lambda b, s: (b, s, 0)),
        scratch_shapes=[pltpu.VMEM((RET_HEADS, 128, RET_V_DIM), F32)],
        compiler_params=_params(("parallel", "arbitrary")),
        name="retention",
    )(p3, cosf, sinf, din, qd, kd, cd)


def _sgu_kernel(p_ref, lng_ref, lnb_ref, ws_ref, bs_ref, o_ref, *, ts):
    C = SG_CHUNK
    ti = lax.broadcasted_iota(jnp.int32, (C, C), 0)
    tj = lax.broadcasted_iota(jnp.int32, (C, C), 1)
    causal = ti >= tj
    for c in range(ts // C):
        rows = slice(c * C, (c + 1) * C)
        p = p_ref[rows, :]
        z = 0.5 * p * (1.0 + lax.erf(p * (2.0 ** -0.5)))
        u = z[:, 0:SG_WIDTH]
        vf = z[:, SG_WIDTH:2 * SG_WIDTH]
        mean = jnp.mean(vf, axis=-1, keepdims=True)
        d = vf - mean
        var = jnp.mean(d * d, axis=-1, keepdims=True)
        vn = d * lax.rsqrt(var + EPS) * lng_ref[...] + lnb_ref[...]
        for g in range(SG_GROUPS):
            cols = slice(g * 128, (g + 1) * 128)
            w = jnp.where(causal, ws_ref[g], 0.0)
            mixed = _dot(w, vn[:, cols]) + bs_ref[g]
            o_ref[rows, cols] = u[:, cols] * mixed


def _sgu(p3, lng, lnb, ws, bs, ts):
    B, S, _ = p3.shape
    kern = functools.partial(_sgu_kernel, ts=ts)
    return pl.pallas_call(
        kern,
        out_shape=jax.ShapeDtypeStruct((B, S, SG_WIDTH), F32),
        grid=(B, S // ts),
        in_specs=[pl.BlockSpec((None, ts, SG_COLS), lambda b, s: (b, s, 0)),
                  _const_spec((1, SG_WIDTH)), _const_spec((1, SG_WIDTH)),
                  _const_spec((SG_GROUPS, SG_CHUNK, SG_CHUNK)),
                  _const_spec((SG_GROUPS, SG_CHUNK, 128))],
        out_specs=pl.BlockSpec((None, ts, SG_WIDTH), lambda b, s: (b, s, 0)),
        compiler_params=_params(("parallel", "parallel")),
        name="sgu",
    )(p3, lng, lnb, ws, bs)


def _merge_kernel(x_ref, yr_ref, yt_ref, ys_ref, gl_ref, wb_ref, wo_ref, o_ref):
    acc = None
    for b, y_ref in enumerate((yr_ref, yt_ref, ys_ref)):
        proj = _dot(y_ref[...], wb_ref[b])
        gate = _sigmoid(gl_ref[:, b * D_MODEL:(b + 1) * D_MODEL])
        acc = gate * proj if acc is None else acc + gate * proj
    o_ref[...] = x_ref[...] + _dot(acc, wo_ref[...])


def _merge(x2, yr, yt, ys, gl, wb, wo, tm):
    T, D = x2.shape
    row = lambda n: pl.BlockSpec((tm, n), lambda i: (i, 0))
    return pl.pallas_call(
        _merge_kernel,
        out_shape=jax.ShapeDtypeStruct((T, D), F32),
        grid=(T // tm,),
        in_specs=[row(D), row(512), row(512), row(512), row(GATE_COLS),
                  _const_spec((N_BRANCH, 512, D)), _const_spec((D, D))],
        out_specs=row(D),
        compiler_params=_params(("parallel",)),
        name="merge",
    )(x2, yr, yt, ys, gl, wb, wo)


def _ffn_kernel(x_ref, g_ref, wug_ref, wuv_ref, cwg_ref, cwv_ref, cbg_ref, cbv_ref, wd_ref, fg_ref,
                o_ref, h_s, acc_s, cg_s, cv_s, *, ts, nf, final):
    s = pl.program_id(1)
    f = pl.program_id(2)

    @pl.when(f == 0)
    def _():
        h_s[...] = _rmsnorm(x_ref[...], g_ref[...]).astype(BF16)
        acc_s[...] = jnp.zeros_like(acc_s)

    @pl.when(jnp.logical_and(s == 0, f == 0))
    def _():
        cg_s[...] = jnp.zeros_like(cg_s)
        cv_s[...] = jnp.zeros_like(cv_s)

    row = lax.broadcasted_iota(jnp.int32, (ts, 1), 0)

    def conv(u, carry_ref, cw_ref, cb_ref):
        prev = carry_ref[f]
        u1 = jnp.where(row == 0, prev[7:8, :], pltpu.roll(u, 1, axis=0))
        u2 = jnp.where(row == 0, prev[6:7, :],
                       jnp.where(row == 1, prev[7:8, :], pltpu.roll(u, 2, axis=0)))
        carry_ref[f] = u[ts - 8:ts, :]
        cw = cw_ref[...]
        return cb_ref[...] + cw[0:1, :] * u2 + cw[1:2, :] * u1 + cw[2:3, :] * u

    h = h_s[...]
    ug = jnp.dot(h, wug_ref[...], preferred_element_type=F32)
    uv = jnp.dot(h, wuv_ref[...], preferred_element_type=F32)
    cg = conv(ug, cg_s, cwg_ref, cbg_ref)
    cv = conv(uv, cv_s, cwv_ref, cbv_ref)
    act = cg * _sigmoid(cg) * cv
    acc_s[...] += _dot(act, wd_ref[...])

    @pl.when(f == nf - 1)
    def _():
        y = x_ref[...] + acc_s[...]
        if final:
            y = _rmsnorm(y, fg_ref[...])
        o_ref[...] = y


def _ffn(x3, g, wug, wuv, cwg, cwv, cbg, cbv, wd, fg, ts, tf, final):
    B, S, D = x3.shape
    nf = D_FF // tf
    kern = functools.partial(_ffn_kernel, ts=ts, nf=nf, final=final)
    return pl.pallas_call(
        kern,
        out_shape=jax.ShapeDtypeStruct((B, S, D), F32),
        grid=(B, S // ts, nf),
        in_specs=[pl.BlockSpec((None, ts, D), lambda b, s, f: (b, s, 0)),
                  pl.BlockSpec((1, D), lambda b, s, f: (0, 0)),
                  pl.BlockSpec((D, tf), lambda b, s, f: (0, f)),
                  pl.BlockSpec((D, tf), lambda b, s, f: (0, f)),
                  pl.BlockSpec((CONV_WIDTH, tf), lambda b, s, f: (0, f)),
                  pl.BlockSpec((CONV_WIDTH, tf), lambda b, s, f: (0, f)),
                  pl.BlockSpec((1, tf), lambda b, s, f: (0, f)),
                  pl.BlockSpec((1, tf), lambda b, s, f: (0, f)),
                  pl.BlockSpec((tf, D), lambda b, s, f: (f, 0)),
                  pl.BlockSpec((1, D), lambda b, s, f: (0, 0))],
        out_specs=pl.BlockSpec((None, ts, D), lambda b, s, f: (b, s, 0)),
        scratch_shapes=[pltpu.VMEM((ts, D), BF16), pltpu.VMEM((ts, D), F32),
                        pltpu.VMEM((nf, 8, tf), F32), pltpu.VMEM((nf, 8, tf), F32)],
        compiler_params=_params(("parallel", "arbitrary", "arbitrary")),
        name="conv_ffn",
    )(x3, g, wug, wuv, cwg, cwv, cbg, cbv, wd, fg)


def _tiles(S):
    pick = lambda pref: min(pref, S)
    return dict(inproj_tm=pick(512), rwkv_ts=pick(512), ret_ts=pick(512), sgu_ts=pick(512),
                merge_tm=pick(512), ffn_ts=pick(512))


def _retention_tables(S):
    inv_freq = 1.0 / (ROPE_BASE ** jnp.linspace(0.0, 1.0, RET_QK_DIM // 2, dtype=F32))
    ang = jnp.arange(S, dtype=F32)[:, None] * inv_freq[None, :]
    cos, sin = jnp.cos(ang), jnp.sin(ang)
    cosf = jnp.tile(cos, (1, 2 * RET_HEADS))
    sinf = jnp.tile(jnp.concatenate([-sin, sin], axis=1), (1, RET_HEADS))
    log_gamma = jnp.log(1.0 - 2.0 ** (-5.0 - jnp.arange(RET_HEADS, dtype=F32)))
    pos = jnp.arange(RET_CHUNK, dtype=F32)
    diff = pos[:, None] - pos[None, :]
    din = jnp.where(diff[None] >= 0,
                    jnp.exp(jnp.maximum(diff, 0.0)[None] * log_gamma[:, None, None]), 0.0)
    k_dec = jnp.exp((RET_CHUNK - 1.0 - pos)[None] * log_gamma[:, None])
    q_dec = jnp.exp((pos + 1.0)[None] * log_gamma[:, None])
    c_dec = jnp.exp(RET_CHUNK * log_gamma)
    bc = lambda t: jnp.broadcast_to(t[:, :, None], (RET_HEADS, RET_CHUNK, 128))
    cd = jnp.broadcast_to(c_dec[:, None, None], (RET_HEADS, RET_CHUNK, 128))
    return cosf, sinf, din, bc(q_dec), bc(k_dec), cd


def kernel(x, norm1_g, w_in, rw_mu, rw_w0, rw_w_up, rw_a0, rw_a_up, rw_g_up, rw_k_k, rw_k_a, rw_r_k,
           rw_lnx_g, rw_lnx_b, sg_ln_g, sg_ln_b, sg_w_s, sg_b, w_branch, w_out, norm2_g, ffn_w_up,
           ffn_conv_w, ffn_conv_b, ffn_w_down, final_g):
    B, S, D = x.shape
    T = B * S
    depth = w_in.shape[0]
    t = _tiles(S)
    cosf, sinf, din, qd, kd, cd = _retention_tables(S)
    row = lambda a: a.reshape(1, -1)
    c0, c1, c2 = RW_COLS, RW_COLS + RET_COLS, RW_COLS + RET_COLS + SG_COLS
    zeros = jnp.zeros((RW_DECAY_RANK, RW_WIDTH), F32)
    for l in range(depth):
        x2 = x.reshape(T, D)
        w = w_in[l].astype(BF16)
        g1 = row(norm1_g[l])
        p_rw = _inproj(x2, g1, w[:, :c0], t["inproj_tm"], RW_COLS).reshape(B, S, RW_COLS)
        p_ret = _inproj(x2, g1, w[:, c0:c1], t["inproj_tm"], RET_COLS).reshape(B, S, RET_COLS)
        p_sg = _inproj(x2, g1, w[:, c1:c2], t["inproj_tm"], SG_COLS).reshape(B, S, SG_COLS)
        gl = _inproj(x2, g1, w[:, c2:], t["inproj_tm"], GATE_COLS)
        wa_up = jnp.concatenate(
            [jnp.concatenate([rw_w_up[l], zeros], axis=1),
             jnp.concatenate([zeros, rw_a_up[l]], axis=1)], axis=0).astype(BF16)
        y_rw = _rwkv(p_rw, row(rw_mu[l]), row(rw_w0[l]), row(rw_a0[l]), wa_up,
                     rw_g_up[l].astype(BF16), row(rw_k_k[l]), row(rw_k_a[l]), row(rw_r_k[l]),
                     row(rw_lnx_g[l]), row(rw_lnx_b[l]), t["rwkv_ts"])
        y_ret = _retention(p_ret, cosf, sinf, din, qd, kd, cd, t["ret_ts"])
        bs = jnp.broadcast_to(sg_b[l][:, :, None], (SG_GROUPS, SG_CHUNK, 128))
        y_sg = _sgu(p_sg, row(sg_ln_g[l]), row(sg_ln_b[l]), sg_w_s[l], bs, t["sgu_ts"])
        x2 = _merge(x2, y_rw.reshape(T, -1), y_ret.reshape(T, -1), y_sg.reshape(T, -1), gl,
                    w_branch[l].astype(BF16), w_out[l].astype(BF16), t["merge_tm"])
        wu = ffn_w_up[l].astype(BF16)
        x = _ffn(x2.reshape(B, S, D), row(norm2_g[l]), wu[:, :D_FF], wu[:, D_FF:],
                 ffn_conv_w[l][:, :D_FF], ffn_conv_w[l][:, D_FF:],
                 row(ffn_conv_b[l][:D_FF]), row(ffn_conv_b[l][D_FF:]),
                 ffn_w_down[l].astype(BF16), row(final_g), t["ffn_ts"], 1408,
                 final=(l == depth - 1))
    return x
```

```python
import functools

import jax
import jax.numpy as jnp
from jax import lax
from jax.experimental import pallas as pl
from jax.experimental.pallas import tpu as pltpu

F32 = jnp.float32
BF16 = jnp.bfloat16

D_MODEL = 1024
RW_HEADS = 8
RW_HEAD_DIM = 64
RW_WIDTH = 512
RW_DECAY_RANK = 64
RW_ICLR_RANK = 64
RW_GATE_RANK = 128
RW_COLS = 1792
RW_GN_EPS = 64e-5
RW_CHUNK = 64
RET_HEADS = 4
RET_QK_DIM = 64
RET_V_DIM = 128
RET_QK_WIDTH = 256
RET_V_WIDTH = 512
RET_COLS = 1536
RET_CHUNK = 128
ROPE_BASE = 10000.0
SG_GROUPS = 4
SG_CHUNK = 128
SG_WIDTH = 512
SG_COLS = 1024
N_BRANCH = 3
GATE_COLS = 3072
D_FF = 2816
CONV_WIDTH = 3
EPS = 1e-6

V7X_LANES = 128
V7X_VMEM_LIMIT_BYTES = 56 * 1024 * 1024


def _dot(a, b):
    return jnp.dot(a.astype(BF16), b.astype(BF16), preferred_element_type=F32)


def _dot_nt(a, b):
    return lax.dot_general(a.astype(BF16), b.astype(BF16), (((1,), (1,)), ((), ())),
                           preferred_element_type=F32)


def _dot_tn(a, b):
    return lax.dot_general(a.astype(BF16), b.astype(BF16), (((0,), (0,)), ((), ())),
                           preferred_element_type=F32)


def _split(a):
    hi = a.astype(BF16)
    lo = (a - hi.astype(F32)).astype(BF16)
    return hi, lo


def _dot_exact_rhs(a, b_bf16):
    hi, lo = _split(a)
    return (jnp.dot(hi, b_bf16, preferred_element_type=F32)
            + jnp.dot(lo, b_bf16, preferred_element_type=F32))


def _dot_exact_lhs(a_bf16, b):
    hi, lo = _split(b)
    return (jnp.dot(a_bf16, hi, preferred_element_type=F32)
            + jnp.dot(a_bf16, lo, preferred_element_type=F32))


def _sigmoid(x):
    return 1.0 / (1.0 + jnp.exp(-x))


def _rmsnorm(x, g):
    return x * lax.rsqrt(jnp.mean(x * x, axis=-1, keepdims=True) + EPS) * g


def _params(sem):
    return pltpu.CompilerParams(dimension_semantics=sem, vmem_limit_bytes=V7X_VMEM_LIMIT_BYTES)


def _const_spec(shape):
    nd = len(shape)
    return pl.BlockSpec(shape, lambda *_: (0,) * nd)


def _inproj_kernel(x_ref, g_ref, w_ref, o_ref, h_ref):
    @pl.when(pl.program_id(1) == 0)
    def _():
        h_ref[...] = _rmsnorm(x_ref[...], g_ref[...]).astype(BF16)

    o_ref[...] = jnp.dot(h_ref[...], w_ref[...], preferred_element_type=F32)


def _inproj(x2, g, w, tm, tn):
    T, D = x2.shape
    N = w.shape[1]
    return pl.pallas_call(
        _inproj_kernel,
        out_shape=jax.ShapeDtypeStruct((T, N), F32),
        grid=(T // tm, N // tn),
        in_specs=[pl.BlockSpec((tm, D), lambda i, j: (i, 0)),
                  pl.BlockSpec((1, D), lambda i, j: (0, 0)),
                  pl.BlockSpec((D, tn), lambda i, j: (0, j))],
        out_specs=pl.BlockSpec((tm, tn), lambda i, j: (i, j)),
        scratch_shapes=[pltpu.VMEM((tm, D), BF16)],
        compiler_params=_params(("parallel", "arbitrary")),
        name="inproj",
    )(x2, g, w)


def _rwkv_kernel(p_ref, mu_ref, w0_ref, a0_ref, wa_up_ref, g_up_ref, kk_ref, ka_ref, rk_ref,
                 lng_ref, lnb_ref, o_ref,
                 carry, hst, r_s, lw_s, k_s, v_s, al_s, be_s, y_s, pend_s, gt_s, nt_s, rp_s, op_s,
                 *, ts):
    C = RW_CHUNK
    W = RW_WIDTH

    @pl.when(pl.program_id(1) == 0)
    def _():
        carry[...] = jnp.zeros_like(carry)
        hst[...] = jnp.zeros_like(hst)

    p = p_ref[...]
    row = lax.broadcasted_iota(jnp.int32, (ts, 1), 0)
    p_prev = jnp.where(row == 0, carry[7:8, :], pltpu.roll(p, 1, axis=0))
    carry[...] = p[ts - 8:ts, :]
    xs = p + mu_ref[...] * (p_prev - p)
    r = xs[:, 0:W]
    k = xs[:, W:2 * W]
    v = xs[:, 2 * W:3 * W]
    wa = xs[:, 3 * W:3 * W + 128]
    gd = xs[:, 3 * W + 128:3 * W + 256]
    lane128 = lax.broadcasted_iota(jnp.int32, (1, 128), 1)
    wa = jnp.where(lane128 < RW_DECAY_RANK, jnp.tanh(wa), wa)
    low = _dot(wa, wa_up_ref[...])
    z = -(w0_ref[...] + low[:, 0:W])
    softplus = jnp.maximum(z, 0.0) + jnp.log(1.0 + jnp.exp(-jnp.abs(z)))
    lw = -jnp.exp(-softplus - 0.5)
    a = _sigmoid(a0_ref[...] + low[:, W:2 * W])
    g = _dot(_sigmoid(gd), g_up_ref[...])

    ri = lax.broadcasted_iota(jnp.int32, (W, W), 0) // RW_HEAD_DIM
    ci = lax.broadcasted_iota(jnp.int32, (W, W), 1) // RW_HEAD_DIM
    headsum = jnp.where(ri == ci, 1.0, 0.0).astype(BF16)

    kk = k * kk_ref[...]
    ss = _dot_exact_rhs(kk * kk, headsum)
    kk = kk / jnp.maximum(jnp.sqrt(ss), 1e-12)
    k2 = k * (1.0 + (a - 1.0) * ka_ref[...])
    r_s[...] = r
    lw_s[...] = lw
    k_s[...] = k2
    v_s[...] = v
    al_s[...] = -kk
    be_s[...] = kk * a

    NP = RW_HEADS // 2
    C2 = 2 * C
    HD2 = 2 * RW_HEAD_DIM
    ti = lax.broadcasted_iota(jnp.int32, (C, C), 0)
    tj = lax.broadcasted_iota(jnp.int32, (C, C), 1)
    tri_incl = jnp.where(ti >= tj, 1.0, 0.0).astype(BF16)
    si = lax.broadcasted_iota(jnp.int32, (C2, C2), 0)
    sj = lax.broadcasted_iota(jnp.int32, (C2, C2), 1)
    same_rows = (si // C) == (sj // C)
    lower_strict = jnp.logical_and(same_rows, (si % C) > (sj % C))
    lower_incl = jnp.logical_and(same_rows, (si % C) >= (sj % C))
    hi_ = lax.broadcasted_iota(jnp.int32, (HD2, HD2), 0) // RW_HEAD_DIM
    hj_ = lax.broadcasted_iota(jnp.int32, (HD2, HD2), 1) // RW_HEAD_DIM
    same_head = hi_ == hj_
    first_half = lane128 < RW_HEAD_DIM
    second_half = jnp.logical_not(first_half)
    pairs = range(NP)
    cols = [slice(pr * HD2, (pr + 1) * HD2) for pr in pairs]

    def stack2(t):
        return jnp.concatenate([jnp.where(first_half, t, 0.0), jnp.where(second_half, t, 0.0)], axis=0)

    def phase1(c, _):
        sl = pl.ds(pl.multiple_of(c * C, C), C)
        lwc = lw_s[sl, :]
        logp = _dot_exact_lhs(tri_incl, lwc)
        logp_last = logp[C - 1:C, :]
        e_p = jnp.exp(logp)
        e_n = jnp.exp(-logp)
        e_x = jnp.exp(logp - lwc)
        e_l = jnp.exp(logp_last - logp)
        pend_s[c] = jnp.exp(logp_last)
        rc = r_s[sl, :]
        kc = k_s[sl, :]
        vc = v_s[sl, :]
        alc = al_s[sl, :]
        bec = be_s[sl, :]
        rb = rc * e_p
        ab = alc * e_x
        bt = bec * e_n
        kt = kc * e_n
        btp = bec * e_l
        ktp = kc * e_l
        a_st = [stack2(ab[:, ls]) for ls in cols]
        r_st = [stack2(rb[:, ls]) for ls in cols]
        b_st = [stack2(bt[:, ls]).astype(BF16) for ls in cols]
        k_st = [stack2(kt[:, ls]).astype(BF16) for ls in cols]
        bp_st = [stack2(btp[:, ls]).astype(BF16) for ls in cols]
        vp = [vc[:, ls].astype(BF16) for ls in cols]
        v_st = [jnp.concatenate([t, t], axis=0) for t in vp]
        smat = [_dot_nt(jnp.concatenate([a_st[i], r_st[i]], axis=0),
                        jnp.concatenate([b_st[i], k_st[i]], axis=0)) for i in pairs]
        s_ab = [jnp.where(lower_strict, m[0:C2, 0:C2], 0.0) for m in smat]
        s_ak = [jnp.where(lower_strict, m[0:C2, C2:2 * C2], 0.0) for m in smat]
        s_rb = [jnp.where(lower_incl, m[C2:2 * C2, 0:C2], 0.0) for m in smat]
        s_rk = [jnp.where(lower_incl, m[C2:2 * C2, C2:2 * C2], 0.0) for m in smat]
        sakv = [_dot(s_ak[i], v_st[i]) for i in pairs]
        y = [jnp.concatenate([a_st[i], sakv[i]], axis=1) for i in pairs]
        pm = s_ab
        y = [y[i] + _dot(pm[i], y[i]) for i in pairs]
        for _ in range(5):
            pm = [_dot(t, t) for t in pm]
            y = [y[i] + _dot(pm[i], y[i]) for i in pairs]
        srkv = [_dot(s_rk[i], v_st[i]) for i in pairs]
        ro = [jnp.concatenate([r_st[i], srkv[i]], axis=1) + _dot(s_rb[i], y[i]) for i in pairs]
        gn = [_dot_tn(y[i], bp_st[i]) for i in pairs]
        vk = [_dot_tn(vp[i], ktp[:, cols[i]]) for i in pairs]
        for i in pairs:
            gt_s[c, i] = gn[i][0:HD2, :].astype(BF16)
            nt_s[c, i] = jnp.where(same_head, gn[i][HD2:2 * HD2, :] + vk[i], 0.0)
            rp_s[c, i] = ro[i][:, 0:HD2].astype(BF16)
            op_s[c, i] = jnp.where(first_half, ro[i][0:C, HD2:2 * HD2], ro[i][C:C2, HD2:2 * HD2])
        return 0

    lax.fori_loop(0, ts // C, phase1, 0)

    for c in range(ts // C):
        pend = pend_s[c]
        for i in pairs:
            hs = hst[i]
            hb = hs.astype(BF16)
            o_st = _dot_nt(rp_s[c, i], hb)
            y_s[c * C:(c + 1) * C, cols[i]] = o_st[0:C] + o_st[C:C2] + op_s[c, i]
            hst[i] = (hs * pend[:, cols[i]]
                      + jnp.dot(hb, gt_s[c, i], preferred_element_type=F32) + nt_s[c, i])


    y = y_s[...]
    inv_n = 1.0 / RW_HEAD_DIM
    mean = _dot_exact_rhs(y, headsum) * inv_n
    d = y - mean
    var = _dot_exact_rhs(d * d, headsum) * inv_n
    yn = d * lax.rsqrt(var + RW_GN_EPS) * lng_ref[...] + lnb_ref[...]
    bonus = _dot_exact_rhs(r * k2 * rk_ref[...], headsum) * v
    o_ref[...] = (yn + bonus) * g


def _rwkv(p3, mu, w0, a0, wa_up, g_up, k_k, k_a, r_k, lng, lnb, ts):
    B, S, _ = p3.shape
    kern = functools.partial(_rwkv_kernel, ts=ts)
    vec = lambda n: _const_spec((1, n))
    tile = lambda: pltpu.VMEM((ts, RW_WIDTH), F32)
    nc = ts // RW_CHUNK
    return pl.pallas_call(
        kern,
        out_shape=jax.ShapeDtypeStruct((B, S, RW_WIDTH), F32),
        grid=(B, S // ts),
        in_specs=[pl.BlockSpec((None, ts, RW_COLS), lambda b, s: (b, s, 0)),
                  vec(RW_COLS), vec(RW_WIDTH), vec(RW_WIDTH),
                  _const_spec((128, 2 * RW_WIDTH)), _const_spec((RW_GATE_RANK, RW_WIDTH)),
                  vec(RW_WIDTH), vec(RW_WIDTH), vec(RW_WIDTH), vec(RW_WIDTH), vec(RW_WIDTH)],
        out_specs=pl.BlockSpec((None, ts, RW_WIDTH), lambda b, s: (b, s, 0)),
        scratch_shapes=[pltpu.VMEM((8, RW_COLS), F32),
                        pltpu.VMEM((RW_HEADS // 2, 128, 128), F32),
                        tile(), tile(), tile(), tile(), tile(), tile(), tile(),
                        pltpu.VMEM((nc, 1, RW_WIDTH), F32),
                        pltpu.VMEM((nc, RW_HEADS // 2, 128, 128), BF16),
                        pltpu.VMEM((nc, RW_HEADS // 2, 128, 128), F32),
                        pltpu.VMEM((nc, RW_HEADS // 2, 2 * RW_CHUNK, 128), BF16),
                        pltpu.VMEM((nc, RW_HEADS // 2, RW_CHUNK, 128), F32)],
        compiler_params=_params(("parallel", "arbitrary")),
        name="rwkv7",
    )(p3, mu, w0, a0, wa_up, g_up, k_k, k_a, r_k, lng, lnb)


def _ret_kernel(p_ref, cos_ref, sin_ref, din_ref, qd_ref, kd_ref, cd_ref, o_ref, rst, *, ts):
    C = RET_CHUNK

    @pl.when(pl.program_id(1) == 0)
    def _():
        rst[...] = jnp.zeros_like(rst)

    lane = lax.broadcasted_iota(jnp.int32, (1, RET_QK_WIDTH), 1)
    first = (lane % RET_QK_DIM) < (RET_QK_DIM // 2)
    lane128 = lax.broadcasted_iota(jnp.int32, (1, 128), 1)
    first_head = lane128 < RET_QK_DIM

    def rot(t, cs, sn):
        sw = jnp.where(first, pltpu.roll(t, RET_QK_WIDTH - RET_QK_DIM // 2, axis=1),
                       pltpu.roll(t, RET_QK_DIM // 2, axis=1))
        return t * cs + sw * sn

    for c in range(ts // C):
        rows = slice(c * C, (c + 1) * C)
        cs = cos_ref[rows, :]
        sn = sin_ref[rows, :]
        q = rot(p_ref[rows, 0:RET_QK_WIDTH], cs, sn)
        k = rot(p_ref[rows, RET_QK_WIDTH:2 * RET_QK_WIDTH], cs, sn) * (RET_QK_DIM ** -0.5)
        for h in range(RET_HEADS):
            pr = h // 2
            msk = first_head if h % 2 == 0 else jnp.logical_not(first_head)
            qm = jnp.where(msk, q[:, pr * 128:(pr + 1) * 128], 0.0)
            kp = k[:, pr * 128:(pr + 1) * 128]
            vh = p_ref[rows, 2 * RET_QK_WIDTH + h * RET_V_DIM:2 * RET_QK_WIDTH + (h + 1) * RET_V_DIM]
            gh = p_ref[rows, 2 * RET_QK_WIDTH + RET_V_WIDTH + h * RET_V_DIM:
                       2 * RET_QK_WIDTH + RET_V_WIDTH + (h + 1) * RET_V_DIM]
            scores = _dot_nt(qm, kp) * din_ref[h]
            inner = _dot(scores, vh)
            rh = rst[h]
            cross = _dot(qm, rh) * qd_ref[h]
            y = inner + cross
            km = jnp.where(msk, kp, 0.0) * kd_ref[h]
            rst[h] = rh * cd_ref[h] + _dot_tn(km, vh)
            y = y * lax.rsqrt(jnp.mean(y * y, axis=-1, keepdims=True) + EPS)
            o_ref[rows, h * RET_V_DIM:(h + 1) * RET_V_DIM] = y * (gh * _sigmoid(gh))


def _retention(p3, cosf, sinf, din, qd, kd, cd, ts):
    B, S, _ = p3.shape
    kern = functools.partial(_ret_kernel, ts=ts)
    tab = lambda: _const_spec((RET_HEADS, RET_CHUNK, 128))
    return pl.pallas_call(
        kern,
        out_shape=jax.ShapeDtypeStruct((B, S, RET_V_WIDTH), F32),
        grid=(B, S // ts),
        in_specs=[pl.BlockSpec((None, ts, RET_COLS), lambda b, s: (b, s, 0)),
                  pl.BlockSpec((ts, RET_QK_WIDTH), lambda b, s: (s, 0)),
                  pl.BlockSpec((ts, RET_QK_WIDTH), lambda b, s: (s, 0)),
                  tab(), tab(), tab(), tab()],
        out_specs=pl.BlockSpec((None, ts, RET_V_WIDTH), lambda b, s: (b, s, 0)),
        scratch_shapes=[pltpu.VMEM((RET_HEADS, 128, RET_V_DIM), F32)],
        compiler_params=_params(("parallel", "arbitrary")),
        name="retention",
    )(p3, cosf, sinf, din, qd, kd, cd)


def _sgu_kernel(p_ref, lng_ref, lnb_ref, ws_ref, bs_ref, o_ref, *, ts):
    C = SG_CHUNK
    ti = lax.broadcasted_iota(jnp.int32, (C, C), 0)
    tj = lax.broadcasted_iota(jnp.int32, (C, C), 1)
    causal = ti >= tj
    for c in range(ts // C):
        rows = slice(c * C, (c + 1) * C)
        p = p_ref[rows, :]
        z = 0.5 * p * (1.0 + lax.erf(p * (2.0 ** -0.5)))
        u = z[:, 0:SG_WIDTH]
        vf = z[:, SG_WIDTH:2 * SG_WIDTH]
        mean = jnp.mean(vf, axis=-1, keepdims=True)
        d = vf - mean
        var = jnp.mean(d * d, axis=-1, keepdims=True)
        vn = d * lax.rsqrt(var + EPS) * lng_ref[...] + lnb_ref[...]
        for g in range(SG_GROUPS):
            cols = slice(g * 128, (g + 1) * 128)
            w = jnp.where(causal, ws_ref[g], 0.0)
            mixed = _dot(w, vn[:, cols]) + bs_ref[g]
            o_ref[rows, cols] = u[:, cols] * mixed


def _sgu(p3, lng, lnb, ws, bs, ts):
    B, S, _ = p3.shape
    kern = functools.partial(_sgu_kernel, ts=ts)
    return pl.pallas_call(
        kern,
        out_shape=jax.ShapeDtypeStruct((B, S, SG_WIDTH), F32),
        grid=(B, S // ts),
        in_specs=[pl.BlockSpec((None, ts, SG_COLS), lambda b, s: (b, s, 0)),
                  _const_spec((1, SG_WIDTH)), _const_spec((1, SG_WIDTH)),
                  _const_spec((SG_GROUPS, SG_CHUNK, SG_CHUNK)),
                  _const_spec((SG_GROUPS, SG_CHUNK, 128))],
        out_specs=pl.BlockSpec((None, ts, SG_WIDTH), lambda b, s: (b, s, 0)),
        compiler_params=_params(("parallel", "parallel")),
        name="sgu",
    )(p3, lng, lnb, ws, bs)


def _merge_kernel(x_ref, yr_ref, yt_ref, ys_ref, gl_ref, wb_ref, wo_ref, o_ref):
    acc = None
    for b, y_ref in enumerate((yr_ref, yt_ref, ys_ref)):
        proj = _dot(y_ref[...], wb_ref[b])
        gate = _sigmoid(gl_ref[:, b * D_MODEL:(b + 1) * D_MODEL])
        acc = gate * proj if acc is None else acc + gate * proj
    o_ref[...] = x_ref[...] + _dot(acc, wo_ref[...])


def _merge(x2, yr, yt, ys, gl, wb, wo, tm):
    T, D = x2.shape
    row = lambda n: pl.BlockSpec((tm, n), lambda i: (i, 0))
    return pl.pallas_call(
        _merge_kernel,
        out_shape=jax.ShapeDtypeStruct((T, D), F32),
        grid=(T // tm,),
        in_specs=[row(D), row(512), row(512), row(512), row(GATE_COLS),
                  _const_spec((N_BRANCH, 512, D)), _const_spec((D, D))],
        out_specs=row(D),
        compiler_params=_params(("parallel",)),
        name="merge",
    )(x2, yr, yt, ys, gl, wb, wo)


def _ffn_kernel(x_ref, g_ref, wug_ref, wuv_ref, cwg_ref, cwv_ref, cbg_ref, cbv_ref, wd_ref, fg_ref,
                o_ref, h_s, acc_s, cg_s, cv_s, *, ts, nf, final):
    s = pl.program_id(1)
    f = pl.program_id(2)

    @pl.when(f == 0)
    def _():
        h_s[...] = _rmsnorm(x_ref[...], g_ref[...]).astype(BF16)
        acc_s[...] = jnp.zeros_like(acc_s)

    @pl.when(jnp.logical_and(s == 0, f == 0))
    def _():
        cg_s[...] = jnp.zeros_like(cg_s)
        cv_s[...] = jnp.zeros_like(cv_s)

    row = lax.broadcasted_iota(jnp.int32, (ts, 1), 0)

    def conv(u, carry_ref, cw_ref, cb_ref):
        prev = carry_ref[f]
        u1 = jnp.where(row == 0, prev[7:8, :], pltpu.roll(u, 1, axis=0))
        u2 = jnp.where(row == 0, prev[6:7, :],
                       jnp.where(row == 1, prev[7:8, :], pltpu.roll(u, 2, axis=0)))
        carry_ref[f] = u[ts - 8:ts, :]
        cw = cw_ref[...]
        return cb_ref[...] + cw[0:1, :] * u2 + cw[1:2, :] * u1 + cw[2:3, :] * u

    h = h_s[...]
    ug = jnp.dot(h, wug_ref[...], preferred_element_type=F32)
    uv = jnp.dot(h, wuv_ref[...], preferred_element_type=F32)
    cg = conv(ug, cg_s, cwg_ref, cbg_ref)
    cv = conv(uv, cv_s, cwv_ref, cbv_ref)
    act = cg * _sigmoid(cg) * cv
    acc_s[...] += _dot(act, wd_ref[...])

    @pl.when(f == nf - 1)
    def _():
        y = x_ref[...] + acc_s[...]
        if final:
            y = _rmsnorm(y, fg_ref[...])
        o_ref[...] = y


def _ffn(x3, g, wug, wuv, cwg, cwv, cbg, cbv, wd, fg, ts, tf, final):
    B, S, D = x3.shape
    nf = D_FF // tf
    kern = functools.partial(_ffn_kernel, ts=ts, nf=nf, final=final)
    return pl.pallas_call(
        kern,
        out_shape=jax.ShapeDtypeStruct((B, S, D), F32),
        grid=(B, S // ts, nf),
        in_specs=[pl.BlockSpec((None, ts, D), lambda b, s, f: (b, s, 0)),
                  pl.BlockSpec((1, D), lambda b, s, f: (0, 0)),
                  pl.BlockSpec((D, tf), lambda b, s, f: (0, f)),
                  pl.BlockSpec((D, tf), lambda b, s, f: (0, f)),
                  pl.BlockSpec((CONV_WIDTH, tf), lambda b, s, f: (0, f)),
                  pl.BlockSpec((CONV_WIDTH, tf), lambda b, s, f: (0, f)),
                  pl.BlockSpec((1, tf), lambda b, s, f: (0, f)),
                  pl.BlockSpec((1, tf), lambda b, s, f: (0, f)),
                  pl.BlockSpec((tf, D), lambda b, s, f: (f, 0)),
                  pl.BlockSpec((1, D), lambda b, s, f: (0, 0))],
        out_specs=pl.BlockSpec((None, ts, D), lambda b, s, f: (b, s, 0)),
        scratch_shapes=[pltpu.VMEM((ts, D), BF16), pltpu.VMEM((ts, D), F32),
                        pltpu.VMEM((nf, 8, tf), F32), pltpu.VMEM((nf, 8, tf), F32)],
        compiler_params=_params(("parallel", "arbitrary", "arbitrary")),
        name="conv_ffn",
    )(x3, g, wug, wuv, cwg, cwv, cbg, cbv, wd, fg)


def _tiles(S):
    pick = lambda pref: min(pref, S)
    return dict(inproj_tm=pick(512), rwkv_ts=pick(512), ret_ts=pick(512), sgu_ts=pick(512),
                merge_tm=pick(512), ffn_ts=pick(512))


def _retention_tables(S):
    inv_freq = 1.0 / (ROPE_BASE ** jnp.linspace(0.0, 1.0, RET_QK_DIM // 2, dtype=F32))
    ang = jnp.arange(S, dtype=F32)[:, None] * inv_freq[None, :]
    cos, sin = jnp.cos(ang), jnp.sin(ang)
    cosf = jnp.tile(cos, (1, 2 * RET_HEADS))
    sinf = jnp.tile(jnp.concatenate([-sin, sin], axis=1), (1, RET_HEADS))
    log_gamma = jnp.log(1.0 - 2.0 ** (-5.0 - jnp.arange(RET_HEADS, dtype=F32)))
    pos = jnp.arange(RET_CHUNK, dtype=F32)
    diff = pos[:, None] - pos[None, :]
    din = jnp.where(diff[None] >= 0,
                    jnp.exp(jnp.maximum(diff, 0.0)[None] * log_gamma[:, None, None]), 0.0)
    k_dec = jnp.exp((RET_CHUNK - 1.0 - pos)[None] * log_gamma[:, None])
    q_dec = jnp.exp((pos + 1.0)[None] * log_gamma[:, None])
    c_dec = jnp.exp(RET_CHUNK * log_gamma)
    bc = lambda t: jnp.broadcast_to(t[:, :, None], (RET_HEADS, RET_CHUNK, 128))
    cd = jnp.broadcast_to(c_dec[:, None, None], (RET_HEADS, RET_CHUNK, 128))
    return cosf, sinf, din, bc(q_dec), bc(k_dec), cd


def kernel(x, norm1_g, w_in, rw_mu, rw_w0, rw_w_up, rw_a0, rw_a_up, rw_g_up, rw_k_k, rw_k_a, rw_r_k,
           rw_lnx_g, rw_lnx_b, sg_ln_g, sg_ln_b, sg_w_s, sg_b, w_branch, w_out, norm2_g, ffn_w_up,
           ffn_conv_w, ffn_conv_b, ffn_w_down, final_g):
    B, S, D = x.shape
    T = B * S
    depth = w_in.shape[0]
    t = _tiles(S)
    cosf, sinf, din, qd, kd, cd = _retention_tables(S)
    row = lambda a: a.reshape(1, -1)
    c0, c1, c2 = RW_COLS, RW_COLS + RET_COLS, RW_COLS + RET_COLS + SG_COLS
    zeros = jnp.zeros((RW_DECAY_RANK, RW_WIDTH), F32)
    for l in range(depth):
        x2 = x.reshape(T, D)
        w = w_in[l].astype(BF16)
        g1 = row(norm1_g[l])
        p_rw = _inproj(x2, g1, w[:, :c0], t["inproj_tm"], RW_COLS).reshape(B, S, RW_COLS)
        p_ret = _inproj(x2, g1, w[:, c0:c1], t["inproj_tm"], RET_COLS).reshape(B, S, RET_COLS)
        p_sg = _inproj(x2, g1, w[:, c1:c2], t["inproj_tm"], SG_COLS).reshape(B, S, SG_COLS)
        gl = _inproj(x2, g1, w[:, c2:], t["inproj_tm"], GATE_COLS)
        wa_up = jnp.concatenate(
            [jnp.concatenate([rw_w_up[l], zeros], axis=1),
             jnp.concatenate([zeros, rw_a_up[l]], axis=1)], axis=0).astype(BF16)
        y_rw = _rwkv(p_rw, row(rw_mu[l]), row(rw_w0[l]), row(rw_a0[l]), wa_up,
                     rw_g_up[l].astype(BF16), row(rw_k_k[l]), row(rw_k_a[l]), row(rw_r_k[l]),
                     row(rw_lnx_g[l]), row(rw_lnx_b[l]), t["rwkv_ts"])
        y_ret = _retention(p_ret, cosf, sinf, din, qd, kd, cd, t["ret_ts"])
        bs = jnp.broadcast_to(sg_b[l][:, :, None], (SG_GROUPS, SG_CHUNK, 128))
        y_sg = _sgu(p_sg, row(sg_ln_g[l]), row(sg_ln_b[l]), sg_w_s[l], bs, t["sgu_ts"])
        x2 = _merge(x2, y_rw.reshape(T, -1), y_ret.reshape(T, -1), y_sg.reshape(T, -1), gl,
                    w_branch[l].astype(BF16), w_out[l].astype(BF16), t["merge_tm"])
        wu = ffn_w_up[l].astype(BF16)
        x = _ffn(x2.reshape(B, S, D), row(norm2_g[l]), wu[:, :D_FF], wu[:, D_FF:],
                 ffn_conv_w[l][:, :D_FF], ffn_conv_w[l][:, D_FF:],
                 row(ffn_conv_b[l][:D_FF]), row(ffn_conv_b[l][D_FF:]),
                 ffn_w_down[l].astype(BF16), row(final_g), t["ffn_ts"], 1408,
                 final=(l == depth - 1))
    return x
```

```python
import functools

import jax
import jax.numpy as jnp
from jax import lax
from jax.experimental import pallas as pl
from jax.experimental.pallas import tpu as pltpu

F32 = jnp.float32
BF16 = jnp.bfloat16

D_MODEL = 1024
RW_HEADS = 8
RW_HEAD_DIM = 64
RW_WIDTH = 512
RW_DECAY_RANK = 64
RW_ICLR_RANK = 64
RW_GATE_RANK = 128
RW_COLS = 1792
RW_GN_EPS = 64e-5
RW_CHUNK = 64
RW_UNROLL = 4
RET_HEADS = 4
RET_QK_DIM = 64
RET_V_DIM = 128
RET_QK_WIDTH = 256
RET_V_WIDTH = 512
RET_COLS = 1536
RET_CHUNK = 128
ROPE_BASE = 10000.0
SG_GROUPS = 4
SG_CHUNK = 128
SG_WIDTH = 512
SG_COLS = 1024
N_BRANCH = 3
GATE_COLS = 3072
IN_SPLITS = (RW_COLS, RET_COLS, SG_COLS, GATE_COLS)
D_FF = 2816
CONV_WIDTH = 3
EPS = 1e-6

V7X_MXU_DIM = 256
V7X_VMEM_LIMIT_BYTES = 60 * 1024 * 1024


def _dot(a, b):
    return jnp.dot(a.astype(BF16), b.astype(BF16), preferred_element_type=F32)


def _dot_nt(a, b):
    return lax.dot_general(a.astype(BF16), b.astype(BF16), (((1,), (1,)), ((), ())),
                           preferred_element_type=F32)


def _dot_tn(a, b):
    return lax.dot_general(a.astype(BF16), b.astype(BF16), (((0,), (0,)), ((), ())),
                           preferred_element_type=F32)


def _split(a):
    hi = a.astype(BF16)
    lo = (a - hi.astype(F32)).astype(BF16)
    return hi, lo


def _dot_exact_lhs(a_bf16, b):
    hi, lo = _split(b)
    return (jnp.dot(a_bf16, hi, preferred_element_type=F32)
            + jnp.dot(a_bf16, lo, preferred_element_type=F32))


def _sigmoid(x):
    return 0.5 + 0.5 * jnp.tanh(0.5 * x)


def _rmsnorm(x, g):
    return x * lax.rsqrt(jnp.mean(x * x, axis=-1, keepdims=True) + EPS) * g


def _params(sem):
    return pltpu.CompilerParams(dimension_semantics=sem, vmem_limit_bytes=V7X_VMEM_LIMIT_BYTES)


def _const_spec(shape):
    nd = len(shape)
    return pl.BlockSpec(shape, lambda *_: (0,) * nd, pipeline_mode=pl.Buffered(1))


def _inproj_kernel(x_ref, g_ref, w_ref, *o_refs):
    h = _rmsnorm(x_ref[...], g_ref[...]).astype(BF16)
    c0 = 0
    for o_ref in o_refs:
        n = o_ref.shape[-1]
        for j in range(0, n, V7X_MXU_DIM):
            o_ref[:, j:j + V7X_MXU_DIM] = jnp.dot(
                h, w_ref[:, c0 + j:c0 + j + V7X_MXU_DIM], preferred_element_type=F32).astype(BF16)
        c0 += n


def _inproj(x2, g, w, tm):
    T, D = x2.shape
    return pl.pallas_call(
        _inproj_kernel,
        out_shape=[jax.ShapeDtypeStruct((T, n), BF16) for n in IN_SPLITS],
        grid=(T // tm,),
        in_specs=[pl.BlockSpec((tm, D), lambda i: (i, 0)),
                  _const_spec((1, D)), _const_spec(w.shape)],
        out_specs=[pl.BlockSpec((tm, n), lambda i: (i, 0)) for n in IN_SPLITS],
        compiler_params=_params(("parallel",)),
        name="inproj",
    )(x2, g, w)


def _rwkv_kernel(p_ref, mu_ref, w0_ref, a0_ref, wa_up_ref, g_up_ref, kk_ref, ka_ref, rk_ref,
                 lng_ref, lnb_ref, o_ref,
                 carry, hst, r_s, lw_s, k_s, v_s, al_s, be_s, y_s, pend_s, gt_s, nt_s, rp_s, op_s,
                 *, ts):
    C = RW_CHUNK
    W = RW_WIDTH
    HD = RW_HEAD_DIM
    HD2 = 2 * HD

    @pl.when(pl.program_id(1) == 0)
    def _():
        carry[...] = jnp.zeros_like(carry)
        hst[...] = jnp.zeros_like(hst)

    p = p_ref[...].astype(F32)
    row8 = lax.broadcasted_iota(jnp.int32, (8, 1), 0)
    rolled = pltpu.roll(p, 1, axis=0)
    p_prev = jnp.concatenate([jnp.where(row8 == 0, carry[7:8, :], rolled[0:8, :]), rolled[8:, :]],
                             axis=0)
    carry[...] = p[ts - 8:ts, :]
    xs = p + mu_ref[...] * (p_prev - p)
    r = xs[:, 0:W]
    k = xs[:, W:2 * W]
    v = xs[:, 2 * W:3 * W]
    wa = xs[:, 3 * W:3 * W + 128]
    gd = xs[:, 3 * W + 128:3 * W + 256]
    lane128 = lax.broadcasted_iota(jnp.int32, (1, 128), 1)
    wa = jnp.where(lane128 < RW_DECAY_RANK, jnp.tanh(wa), wa)
    low = _dot(wa, wa_up_ref[...])
    z = -(w0_ref[...] + low[:, 0:W])
    softplus = jnp.maximum(z, 0.0) + jnp.log(1.0 + jnp.exp(-jnp.abs(z)))
    lw = -jnp.exp(-softplus - 0.5)
    a = _sigmoid(a0_ref[...] + low[:, W:2 * W])
    g = _dot(_sigmoid(gd), g_up_ref[...])

    hi_ = lax.broadcasted_iota(jnp.int32, (256, 256), 0) // HD
    hj_ = lax.broadcasted_iota(jnp.int32, (256, 256), 1) // HD
    ones_blk = jnp.where(hi_ == hj_, 1.0, 0.0).astype(BF16)

    def head_sums(t):
        tb = t.astype(BF16)
        return jnp.concatenate(
            [jnp.dot(tb[:, j:j + 256], ones_blk, preferred_element_type=F32) for j in (0, 256)],
            axis=1)

    kk = k * kk_ref[...]
    kk = kk * lax.rsqrt(jnp.maximum(head_sums(kk * kk), 1e-24))
    k2 = k * (1.0 + (a - 1.0) * ka_ref[...])
    r_s[...] = r
    lw_s[...] = lw
    k_s[...] = k2
    v_s[...] = v
    al_s[...] = -kk
    be_s[...] = kk * a

    NP = RW_HEADS // 2
    C2 = 2 * C
    ti = lax.broadcasted_iota(jnp.int32, (C, C), 0)
    tj = lax.broadcasted_iota(jnp.int32, (C, C), 1)
    tri_incl = jnp.where(ti >= tj, 1.0, 0.0).astype(BF16)
    si = lax.broadcasted_iota(jnp.int32, (C2, C2), 0)
    sj = lax.broadcasted_iota(jnp.int32, (C2, C2), 1)
    same_rows = (si // C) == (sj // C)
    lower_strict = jnp.logical_and(same_rows, (si % C) > (sj % C))
    lower_incl = jnp.logical_and(same_rows, (si % C) >= (sj % C))
    vi = lax.broadcasted_iota(jnp.int32, (HD2, HD2), 0) // HD
    vj = lax.broadcasted_iota(jnp.int32, (HD2, HD2), 1) // HD
    same_head = vi == vj
    first_half = lane128 < HD
    second_half = jnp.logical_not(first_half)
    pairs = range(NP)
    cols = [slice(pr * HD2, (pr + 1) * HD2) for pr in pairs]

    def stack_own(t):
        return jnp.concatenate([jnp.where(first_half, t, 0.0), jnp.where(second_half, t, 0.0)], axis=0)

    def stack_other(t):
        return jnp.concatenate([jnp.where(second_half, t, 0.0), jnp.where(first_half, t, 0.0)], axis=0)

    def dup(t):
        return jnp.concatenate([t, t], axis=0)

    def chunk_operands(c):
        sl = pl.ds(pl.multiple_of(c * C, C), C)
        lwc = lw_s[sl, :]
        logp = _dot_exact_lhs(tri_incl, lwc)
        logp_last = logp[C - 1:C, :]
        e_p = jnp.exp(logp)
        e_n = jnp.exp(-logp)
        e_x = jnp.exp(logp - lwc)
        e_l = jnp.exp(logp_last - logp)
        pend_s[c] = jnp.exp(logp_last)
        kc = k_s[sl, :]
        bec = be_s[sl, :]
        return dict(rb=r_s[sl, :] * e_p, ab=al_s[sl, :] * e_x, bt=bec * e_n, kt=kc * e_n,
                    btp=bec * e_l, ktp=kc * e_l, v=v_s[sl, :])

    def phase1(it, _):
        chunks = [it * RW_UNROLL + j for j in range(RW_UNROLL)]
        ops = [chunk_operands(c) for c in chunks]
        units = [(j, ls) for j in range(RW_UNROLL) for ls in cols]
        n = range(len(units))
        get = lambda name, u: ops[units[u][0]][name][:, units[u][1]]
        smat = [_dot_nt(jnp.concatenate([get("ab", u), get("rb", u)], axis=0),
                        jnp.concatenate([stack_own(get("bt", u)).astype(BF16),
                                         stack_own(get("kt", u)).astype(BF16)], axis=0))
                for u in n]
        s_ab = [jnp.where(lower_strict, dup(m[0:C, 0:C2]), 0.0) for m in smat]
        s_ak = [jnp.where(lower_strict, dup(m[0:C, C2:2 * C2]), 0.0) for m in smat]
        s_rbk = [jnp.concatenate([jnp.where(lower_incl, dup(m[C:C2, 0:C2]), 0.0),
                                  jnp.where(lower_incl, dup(m[C:C2, C2:2 * C2]), 0.0)],
                                 axis=1).astype(BF16) for m in smat]
        v_o = [stack_other(pltpu.roll(get("v", u), HD, axis=1)).astype(BF16) for u in n]
        y = [stack_own(get("ab", u)) + _dot(s_ak[u], v_o[u]) for u in n]
        pm = s_ab
        for _ in range(5):
            z_ = [_dot(pm[u], jnp.concatenate([pm[u], y[u]], axis=1)) for u in n]
            pm = [t[:, 0:C2] for t in z_]
            y = [y[u] + z_[u][:, C2:2 * C2] for u in n]
        y = [y[u] + _dot(pm[u], y[u]) for u in n]
        yv = [jnp.concatenate([y[u].astype(BF16), v_o[u]], axis=0) for u in n]
        ro = [stack_own(get("rb", u)) + jnp.dot(s_rbk[u], yv[u], preferred_element_type=F32)
              for u in n]
        tn = [_dot_tn(yv[u], jnp.concatenate([stack_own(get("btp", u)).astype(BF16),
                                              stack_own(get("ktp", u)).astype(BF16)], axis=0))
              for u in n]
        for u in n:
            c, i = chunks[units[u][0]], u % NP
            gt_s[c, i] = jnp.where(same_head, tn[u], 0.0).astype(BF16)
            n_sw = jnp.where(same_head, 0.0, tn[u])
            nt_s[c, i] = jnp.concatenate([n_sw[HD:HD2, :], n_sw[0:HD, :]], axis=0)
            rp_s[c, i] = jnp.where(first_half, ro[u][0:C, :], ro[u][C:C2, :]).astype(BF16)
            op_s[c, i] = pltpu.roll(jnp.where(first_half, ro[u][C:C2, :], ro[u][0:C, :]), HD, axis=1)
        return 0

    lax.fori_loop(0, ts // (C * RW_UNROLL), phase1, 0)

    for c in range(ts // C):
        pend = pend_s[c]
        for i in pairs:
            hs = hst[i]
            hb = hs.astype(BF16)
            y_s[c * C:(c + 1) * C, cols[i]] = _dot_nt(rp_s[c, i], hb) + op_s[c, i]
            hst[i] = (hs * pend[:, cols[i]]
                      + jnp.dot(hb, gt_s[c, i], preferred_element_type=F32) + nt_s[c, i])

    y = y_s[...]
    inv_n = 1.0 / HD
    d = y - head_sums(y) * inv_n
    var = head_sums(d * d) * inv_n
    yn = d * lax.rsqrt(var + RW_GN_EPS) * lng_ref[...] + lnb_ref[...]
    bonus = head_sums(r * k2 * rk_ref[...]) * v
    o_ref[...] = ((yn + bonus) * g).astype(BF16)


def _rwkv(p3, mu, w0, a0, wa_up, g_up, k_k, k_a, r_k, lng, lnb, ts):
    B, S, _ = p3.shape
    kern = functools.partial(_rwkv_kernel, ts=ts)
    vec = lambda n: _const_spec((1, n))
    tile = lambda: pltpu.VMEM((ts, RW_WIDTH), F32)
    nc = ts // RW_CHUNK
    npair = RW_HEADS // 2
    return pl.pallas_call(
        kern,
        out_shape=jax.ShapeDtypeStruct((B, S, RW_WIDTH), BF16),
        grid=(B, S // ts),
        in_specs=[pl.BlockSpec((None, ts, RW_COLS), lambda b, s: (b, s, 0)),
                  vec(RW_COLS), vec(RW_WIDTH), vec(RW_WIDTH),
                  _const_spec((128, 2 * RW_WIDTH)), _const_spec((RW_GATE_RANK, RW_WIDTH)),
                  vec(RW_WIDTH), vec(RW_WIDTH), vec(RW_WIDTH), vec(RW_WIDTH), vec(RW_WIDTH)],
        out_specs=pl.BlockSpec((None, ts, RW_WIDTH), lambda b, s: (b, s, 0)),
        scratch_shapes=[pltpu.VMEM((8, RW_COLS), F32),
                        pltpu.VMEM((npair, 128, 128), F32),
                        tile(), tile(), tile(), tile(), tile(), tile(), tile(),
                        pltpu.VMEM((nc, 1, RW_WIDTH), F32),
                        pltpu.VMEM((nc, npair, 128, 128), BF16),
                        pltpu.VMEM((nc, npair, 128, 128), F32),
                        pltpu.VMEM((nc, npair, RW_CHUNK, 128), BF16),
                        pltpu.VMEM((nc, npair, RW_CHUNK, 128), F32)],
        compiler_params=_params(("parallel", "arbitrary")),
        name="rwkv7",
    )(p3, mu, w0, a0, wa_up, g_up, k_k, k_a, r_k, lng, lnb)


def _ret_kernel(p_ref, cos_ref, sin_ref, din_ref, qd_ref, kd_ref, cd_ref, o_ref, rst, *, ts):
    C = RET_CHUNK

    @pl.when(pl.program_id(1) == 0)
    def _():
        rst[...] = jnp.zeros_like(rst)

    lane = lax.broadcasted_iota(jnp.int32, (1, RET_QK_WIDTH), 1)
    first = (lane % RET_QK_DIM) < (RET_QK_DIM // 2)
    lane128 = lax.broadcasted_iota(jnp.int32, (1, 128), 1)
    first_head = lane128 < RET_QK_DIM

    def rot(t, cs, sn):
        sw = jnp.where(first, pltpu.roll(t, RET_QK_WIDTH - RET_QK_DIM // 2, axis=1),
                       pltpu.roll(t, RET_QK_DIM // 2, axis=1))
        return t * cs + sw * sn

    v0 = 2 * RET_QK_WIDTH
    g0 = v0 + RET_V_WIDTH
    for c in range(ts // C):
        rows = slice(c * C, (c + 1) * C)
        cs = cos_ref[rows, :]
        sn = sin_ref[rows, :]
        q = rot(p_ref[rows, 0:RET_QK_WIDTH].astype(F32), cs, sn)
        k = rot(p_ref[rows, RET_QK_WIDTH:2 * RET_QK_WIDTH].astype(F32), cs, sn) * (RET_QK_DIM ** -0.5)
        for h in range(RET_HEADS):
            pr = h // 2
            msk = first_head if h % 2 == 0 else jnp.logical_not(first_head)
            qm = jnp.where(msk, q[:, pr * 128:(pr + 1) * 128], 0.0).astype(BF16)
            kp = k[:, pr * 128:(pr + 1) * 128]
            vh = p_ref[rows, v0 + h * RET_V_DIM:v0 + (h + 1) * RET_V_DIM]
            gh = p_ref[rows, g0 + h * RET_V_DIM:g0 + (h + 1) * RET_V_DIM].astype(F32)
            scores = _dot_nt(qm, kp) * din_ref[h]
            inner = _dot(scores, vh)
            rh = rst[h]
            cross = _dot(qm, rh) * qd_ref[h]
            y = inner + cross
            km = jnp.where(msk, kp, 0.0) * kd_ref[h]
            rst[h] = rh * cd_ref[h] + _dot_tn(km, vh)
            y = y * lax.rsqrt(jnp.mean(y * y, axis=-1, keepdims=True) + EPS)
            o_ref[rows, h * RET_V_DIM:(h + 1) * RET_V_DIM] = (y * (gh * _sigmoid(gh))).astype(BF16)


def _retention(p3, cosf, sinf, din, qd, kd, cd, ts):
    B, S, _ = p3.shape
    kern = functools.partial(_ret_kernel, ts=ts)
    tab = lambda: _const_spec((RET_HEADS, RET_CHUNK, 128))
    return pl.pallas_call(
        kern,
        out_shape=jax.ShapeDtypeStruct((B, S, RET_V_WIDTH), BF16),
        grid=(B, S // ts),
        in_specs=[pl.BlockSpec((None, ts, RET_COLS), lambda b, s: (b, s, 0)),
                  pl.BlockSpec((ts, RET_QK_WIDTH), lambda b, s: (s, 0)),
                  pl.BlockSpec((ts, RET_QK_WIDTH), lambda b, s: (s, 0)),
                  tab(), tab(), tab(), tab()],
        out_specs=pl.BlockSpec((None, ts, RET_V_WIDTH), lambda b, s: (b, s, 0)),
        scratch_shapes=[pltpu.VMEM((RET_HEADS, 128, RET_V_DIM), F32)],
        compiler_params=_params(("parallel", "arbitrary")),
        name="retention",
    )(p3, cosf, sinf, din, qd, kd, cd)


def _sgu_kernel(p_ref, lng_ref, lnb_ref, ws_ref, bs_ref, o_ref, *, ts):
    C = SG_CHUNK
    ti = lax.broadcasted_iota(jnp.int32, (C, C), 0)
    tj = lax.broadcasted_iota(jnp.int32, (C, C), 1)
    causal = ti >= tj
    w = [jnp.where(causal, ws_ref[g], 0.0).astype(BF16) for g in range(SG_GROUPS)]
    for c in range(ts // C):
        rows = slice(c * C, (c + 1) * C)
        p = p_ref[rows, :].astype(F32)
        z = 0.5 * p * (1.0 + lax.erf(p * (2.0 ** -0.5)))
        u = z[:, 0:SG_WIDTH]
        vf = z[:, SG_WIDTH:2 * SG_WIDTH]
        mean = jnp.mean(vf, axis=-1, keepdims=True)
        d = vf - mean
        var = jnp.mean(d * d, axis=-1, keepdims=True)
        vn = d * lax.rsqrt(var + EPS) * lng_ref[...] + lnb_ref[...]
        for g in range(SG_GROUPS):
            gc = slice(g * 128, (g + 1) * 128)
            mixed = _dot(w[g], vn[:, gc]) + bs_ref[g]
            o_ref[rows, gc] = (u[:, gc] * mixed).astype(BF16)


def _sgu(p3, lng, lnb, ws, bs, ts):
    B, S, _ = p3.shape
    kern = functools.partial(_sgu_kernel, ts=ts)
    return pl.pallas_call(
        kern,
        out_shape=jax.ShapeDtypeStruct((B, S, SG_WIDTH), BF16),
        grid=(B, S // ts),
        in_specs=[pl.BlockSpec((None, ts, SG_COLS), lambda b, s: (b, s, 0)),
                  _const_spec((1, SG_WIDTH)), _const_spec((1, SG_WIDTH)),
                  _const_spec((SG_GROUPS, SG_CHUNK, SG_CHUNK)),
                  _const_spec((SG_GROUPS, SG_CHUNK, 128))],
        out_specs=pl.BlockSpec((None, ts, SG_WIDTH), lambda b, s: (b, s, 0)),
        compiler_params=_params(("parallel", "parallel")),
        name="sgu",
    )(p3, lng, lnb, ws, bs)


def _merge_kernel(x_ref, yr_ref, yt_ref, ys_ref, gl_ref, wb_ref, wo_ref, o_ref):
    acc = None
    for b, y_ref in enumerate((yr_ref, yt_ref, ys_ref)):
        proj = jnp.dot(y_ref[...], wb_ref[b], preferred_element_type=F32)
        gate = _sigmoid(gl_ref[:, b * D_MODEL:(b + 1) * D_MODEL].astype(F32))
        acc = gate * proj if acc is None else acc + gate * proj
    o_ref[...] = x_ref[...] + _dot(acc, wo_ref[...])


def _merge(x2, yr, yt, ys, gl, wb, wo, tm):
    T, D = x2.shape
    row = lambda n: pl.BlockSpec((tm, n), lambda i: (i, 0))
    return pl.pallas_call(
        _merge_kernel,
        out_shape=jax.ShapeDtypeStruct((T, D), F32),
        grid=(T // tm,),
        in_specs=[row(D), row(512), row(512), row(512), row(GATE_COLS),
                  _const_spec((N_BRANCH, 512, D)), _const_spec((D, D))],
        out_specs=row(D),
        compiler_params=_params(("parallel",)),
        name="merge",
    )(x2, yr, yt, ys, gl, wb, wo)


def _ffn_kernel(x_ref, g_ref, wu_ref, cw_ref, cb_ref, wd_ref, fg_ref, o_ref, carry, ub, act_s,
                *, ts, final):
    FC = V7X_MXU_DIM
    NFC = D_FF // FC

    @pl.when(pl.program_id(1) == 0)
    def _():
        carry[...] = jnp.zeros_like(carry)

    x = x_ref[...]
    h = _rmsnorm(x, g_ref[...]).astype(BF16)
    col = lambda j, half: slice(half * D_FF + j * FC, half * D_FF + (j + 1) * FC)

    def up(j):
        for half in range(2):
            cs = col(j, half)
            buf = ub.at[2 * (j % 2) + half]
            buf[0:8, :] = carry[:, cs]
            buf[8:8 + ts, :] = jnp.dot(h, wu_ref[:, cs], preferred_element_type=F32)
            carry[:, cs] = buf[ts:ts + 8, :]

    def conv(j, half):
        cs = col(j, half)
        buf = ub.at[2 * (j % 2) + half]
        return (cb_ref[:, cs] + cw_ref[0:1, cs] * buf[6:6 + ts, :] + cw_ref[1:2, cs] * buf[7:7 + ts, :]
                + cw_ref[2:3, cs] * buf[8:8 + ts, :])

    up(0)
    for j in range(NFC):
        if j + 1 < NFC:
            up(j + 1)
        cg = conv(j, 0)
        cv = conv(j, 1)
        act_s[:, j * FC:(j + 1) * FC] = (cg * _sigmoid(cg) * cv).astype(BF16)
    y = x + jnp.dot(act_s[...], wd_ref[...], preferred_element_type=F32)
    if final:
        y = _rmsnorm(y, fg_ref[...])
    o_ref[...] = y


def _ffn(x3, g, wu, cw, cb, wd, fg, ts, final):
    B, S, D = x3.shape
    kern = functools.partial(_ffn_kernel, ts=ts, final=final)
    return pl.pallas_call(
        kern,
        out_shape=jax.ShapeDtypeStruct((B, S, D), F32),
        grid=(B, S // ts),
        in_specs=[pl.BlockSpec((None, ts, D), lambda b, s: (b, s, 0)),
                  _const_spec((1, D)), _const_spec((D, 2 * D_FF)),
                  _const_spec((CONV_WIDTH, 2 * D_FF)), _const_spec((1, 2 * D_FF)),
                  _const_spec((D_FF, D)), _const_spec((1, D))],
        out_specs=pl.BlockSpec((None, ts, D), lambda b, s: (b, s, 0)),
        scratch_shapes=[pltpu.VMEM((8, 2 * D_FF), F32),
                        pltpu.VMEM((4, ts + 8, V7X_MXU_DIM), F32),
                        pltpu.VMEM((ts, D_FF), BF16)],
        compiler_params=_params(("parallel", "arbitrary")),
        name="conv_ffn",
    )(x3, g, wu, cw, cb, wd, fg)


def _tiles(S):
    pick = lambda pref: min(pref, S)
    return dict(inproj_tm=pick(512), rwkv_ts=pick(512), ret_ts=pick(512), sgu_ts=pick(512),
                merge_tm=pick(512), ffn_ts=pick(512))


def _retention_tables(S):
    inv_freq = 1.0 / (ROPE_BASE ** jnp.linspace(0.0, 1.0, RET_QK_DIM // 2, dtype=F32))
    ang = jnp.arange(S, dtype=F32)[:, None] * inv_freq[None, :]
    cos, sin = jnp.cos(ang), jnp.sin(ang)
    cosf = jnp.tile(cos, (1, 2 * RET_HEADS))
    sinf = jnp.tile(jnp.concatenate([-sin, sin], axis=1), (1, RET_HEADS))
    log_gamma = jnp.log(1.0 - 2.0 ** (-5.0 - jnp.arange(RET_HEADS, dtype=F32)))
    pos = jnp.arange(RET_CHUNK, dtype=F32)
    diff = pos[:, None] - pos[None, :]
    din = jnp.where(diff[None] >= 0,
                    jnp.exp(jnp.maximum(diff, 0.0)[None] * log_gamma[:, None, None]), 0.0)
    k_dec = jnp.exp((RET_CHUNK - 1.0 - pos)[None] * log_gamma[:, None])
    q_dec = jnp.exp((pos + 1.0)[None] * log_gamma[:, None])
    c_dec = jnp.exp(RET_CHUNK * log_gamma)
    bc = lambda t: jnp.broadcast_to(t[:, :, None], (RET_HEADS, RET_CHUNK, 128))
    cd = jnp.broadcast_to(c_dec[:, None, None], (RET_HEADS, RET_CHUNK, 128))
    return cosf, sinf, din, bc(q_dec), bc(k_dec), cd


def kernel(x, norm1_g, w_in, rw_mu, rw_w0, rw_w_up, rw_a0, rw_a_up, rw_g_up, rw_k_k, rw_k_a, rw_r_k,
           rw_lnx_g, rw_lnx_b, sg_ln_g, sg_ln_b, sg_w_s, sg_b, w_branch, w_out, norm2_g, ffn_w_up,
           ffn_conv_w, ffn_conv_b, ffn_w_down, final_g):
    B, S, D = x.shape
    T = B * S
    depth = w_in.shape[0]
    t = _tiles(S)
    cosf, sinf, din, qd, kd, cd = _retention_tables(S)
    row = lambda a: a.reshape(1, -1)
    zeros = jnp.zeros((RW_DECAY_RANK, RW_WIDTH), F32)
    for l in range(depth):
        x2 = x.reshape(T, D)
        p_rw, p_ret, p_sg, gl = _inproj(x2, row(norm1_g[l]), w_in[l].astype(BF16), t["inproj_tm"])
        wa_up = jnp.concatenate(
            [jnp.concatenate([rw_w_up[l], zeros], axis=1),
             jnp.concatenate([zeros, rw_a_up[l]], axis=1)], axis=0).astype(BF16)
        y_rw = _rwkv(p_rw.reshape(B, S, -1), row(rw_mu[l]), row(rw_w0[l]), row(rw_a0[l]), wa_up,
                     rw_g_up[l].astype(BF16), row(rw_k_k[l]), row(rw_k_a[l]), row(rw_r_k[l]),
                     row(rw_lnx_g[l]), row(rw_lnx_b[l]), t["rwkv_ts"])
        y_ret = _retention(p_ret.reshape(B, S, -1), cosf, sinf, din, qd, kd, cd, t["ret_ts"])
        bs = jnp.broadcast_to(sg_b[l][:, :, None], (SG_GROUPS, SG_CHUNK, 128))
        y_sg = _sgu(p_sg.reshape(B, S, -1), row(sg_ln_g[l]), row(sg_ln_b[l]), sg_w_s[l], bs,
                    t["sgu_ts"])
        x2 = _merge(x2, y_rw.reshape(T, -1), y_ret.reshape(T, -1), y_sg.reshape(T, -1), gl,
                    w_branch[l].astype(BF16), w_out[l].astype(BF16), t["merge_tm"])
        x = _ffn(x2.reshape(B, S, D), row(norm2_g[l]), ffn_w_up[l].astype(BF16), ffn_conv_w[l],
                 row(ffn_conv_b[l]), ffn_w_down[l].astype(BF16), row(final_g), t["ffn_ts"],
                 final=(l == depth - 1))
    return x
```

```python
import functools

import jax
import jax.numpy as jnp
from jax import lax
from jax.experimental import pallas as pl
from jax.experimental.pallas import tpu as pltpu

F32 = jnp.float32
BF16 = jnp.bfloat16

D_MODEL = 1024
RW_HEADS = 8
RW_HEAD_DIM = 64
RW_WIDTH = 512
RW_DECAY_RANK = 64
RW_ICLR_RANK = 64
RW_GATE_RANK = 128
RW_COLS = 1792
RW_GN_EPS = 64e-5
RW_CHUNK = 64
RW_TILE_CHUNKS = 4
RET_HEADS = 4
RET_QK_DIM = 64
RET_V_DIM = 128
RET_QK_WIDTH = 256
RET_V_WIDTH = 512
RET_COLS = 1536
RET_CHUNK = 128
ROPE_BASE = 10000.0
SG_GROUPS = 4
SG_CHUNK = 128
SG_WIDTH = 512
SG_COLS = 1024
N_BRANCH = 3
GATE_COLS = 3072
IN_SPLITS = (RW_COLS, RET_COLS, SG_COLS, GATE_COLS)
D_FF = 2816
CONV_WIDTH = 3
EPS = 1e-6

V7X_MXU_DIM = 256
V7X_VMEM_LIMIT_BYTES = 60 * 1024 * 1024


def _dot(a, b):
    return jnp.dot(a.astype(BF16), b.astype(BF16), preferred_element_type=F32)


def _dot_nt(a, b):
    return lax.dot_general(a.astype(BF16), b.astype(BF16), (((1,), (1,)), ((), ())),
                           preferred_element_type=F32)


def _dot_tn(a, b):
    return lax.dot_general(a.astype(BF16), b.astype(BF16), (((0,), (0,)), ((), ())),
                           preferred_element_type=F32)


def _split(a):
    hi = a.astype(BF16)
    lo = (a - hi.astype(F32)).astype(BF16)
    return hi, lo


def _dot_exact_lhs(a_bf16, b):
    hi, lo = _split(b)
    return (jnp.dot(a_bf16, hi, preferred_element_type=F32)
            + jnp.dot(a_bf16, lo, preferred_element_type=F32))


def _sigmoid(x):
    return 0.5 + 0.5 * jnp.tanh(0.5 * x)


def _rmsnorm(x, g):
    return x * lax.rsqrt(jnp.mean(x * x, axis=-1, keepdims=True) + EPS) * g


def _params(sem):
    return pltpu.CompilerParams(dimension_semantics=sem, vmem_limit_bytes=V7X_VMEM_LIMIT_BYTES)


def _const_spec(shape):
    nd = len(shape)
    return pl.BlockSpec(shape, lambda *_: (0,) * nd, pipeline_mode=pl.Buffered(1))


def _layer_spec(stacked_shape, layer):
    nd = len(stacked_shape) - 1
    return pl.BlockSpec((None,) + tuple(stacked_shape[1:]), lambda *_: (layer,) + (0,) * nd,
                        pipeline_mode=pl.Buffered(1))


def _interleave(*gens):
    live = list(gens)
    while live:
        for g in list(live):
            try:
                next(g)
            except StopIteration:
                live.remove(g)


def _retention_tile(p_ref, cos_ref, sin_ref, din_ref, qd_ref, kd_ref, cd_ref, o_ref, rst, keep, tm):
    C = RET_CHUNK
    lane = lax.broadcasted_iota(jnp.int32, (1, RET_QK_WIDTH), 1)
    first = (lane % RET_QK_DIM) < (RET_QK_DIM // 2)
    lane128 = lax.broadcasted_iota(jnp.int32, (1, 128), 1)
    first_head = lane128 < RET_QK_DIM

    def rot(t, cs, sn):
        sw = jnp.where(first, pltpu.roll(t, RET_QK_WIDTH - RET_QK_DIM // 2, axis=1),
                       pltpu.roll(t, RET_QK_DIM // 2, axis=1))
        return t * cs + sw * sn

    v0 = 2 * RET_QK_WIDTH
    g0 = v0 + RET_V_WIDTH
    state = [rst[h] * keep for h in range(RET_HEADS)]
    for c in range(tm // C):
        rows = slice(c * C, (c + 1) * C)
        cs = cos_ref[rows, :]
        sn = sin_ref[rows, :]
        q = rot(p_ref[rows, 0:RET_QK_WIDTH].astype(F32), cs, sn)
        k = rot(p_ref[rows, RET_QK_WIDTH:2 * RET_QK_WIDTH].astype(F32), cs, sn) * (RET_QK_DIM ** -0.5)
        yield
        for h in range(RET_HEADS):
            pr = h // 2
            msk = first_head if h % 2 == 0 else jnp.logical_not(first_head)
            qm = jnp.where(msk, q[:, pr * 128:(pr + 1) * 128], 0.0).astype(BF16)
            kp = k[:, pr * 128:(pr + 1) * 128]
            vh = p_ref[rows, v0 + h * RET_V_DIM:v0 + (h + 1) * RET_V_DIM]
            gh = p_ref[rows, g0 + h * RET_V_DIM:g0 + (h + 1) * RET_V_DIM].astype(F32)
            scores = _dot_nt(qm, kp) * din_ref[h]
            y = _dot(scores, vh) + _dot(qm, state[h]) * qd_ref[h]
            km = jnp.where(msk, kp, 0.0) * kd_ref[h]
            state[h] = state[h] * cd_ref[h] + _dot_tn(km, vh)
            y = y * lax.rsqrt(jnp.mean(y * y, axis=-1, keepdims=True) + EPS)
            o_ref[rows, h * RET_V_DIM:(h + 1) * RET_V_DIM] = (y * (gh * _sigmoid(gh))).astype(BF16)
            yield
    for h in range(RET_HEADS):
        rst[h] = state[h]


def _sgu_tile(p_ref, c0, lng_ref, lnb_ref, ws_ref, bs_ref, o_ref, tm):
    C = SG_CHUNK
    ti = lax.broadcasted_iota(jnp.int32, (C, C), 0)
    tj = lax.broadcasted_iota(jnp.int32, (C, C), 1)
    causal = ti >= tj
    w = [jnp.where(causal, ws_ref[g], 0.0).astype(BF16) for g in range(SG_GROUPS)]
    for c in range(tm // C):
        rows = slice(c * C, (c + 1) * C)
        p = p_ref[rows, c0:c0 + SG_COLS].astype(F32)
        z = 0.5 * p * (1.0 + lax.erf(p * (2.0 ** -0.5)))
        u = z[:, 0:SG_WIDTH]
        vf = z[:, SG_WIDTH:2 * SG_WIDTH]
        mean = jnp.mean(vf, axis=-1, keepdims=True)
        d = vf - mean
        var = jnp.mean(d * d, axis=-1, keepdims=True)
        vn = d * lax.rsqrt(var + EPS) * lng_ref[...] + lnb_ref[...]
        yield
        for g in range(SG_GROUPS):
            gc = slice(g * 128, (g + 1) * 128)
            mixed = _dot(w[g], vn[:, gc]) + bs_ref[g]
            o_ref[rows, gc] = (u[:, gc] * mixed).astype(BF16)
        yield


def _inmix_kernel(x_ref, g_ref, w_ref, cos_ref, sin_ref, din_ref, qd_ref, kd_ref, cd_ref,
                  sg_lng_ref, sg_lnb_ref, ws_ref, bs_ref,
                  prw_ref, gl_ref, yret_ref, ysg_ref, cur, prev, rst, *, tm, seq_tiles):
    i = pl.program_id(0)

    @pl.when(i == 0)
    def _():
        prev[...] = jnp.zeros_like(prev)
        rst[...] = jnp.zeros_like(rst)

    h = _rmsnorm(x_ref[...], g_ref[...]).astype(BF16)

    def projection():
        MX = V7X_MXU_DIM
        c0 = 0
        for dst, n in ((prw_ref, RW_COLS), (cur, RET_COLS + SG_COLS), (gl_ref, GATE_COLS)):
            for j in range(0, n, MX):
                dst[:, j:j + MX] = jnp.dot(h, w_ref[:, c0 + j:c0 + j + MX],
                                           preferred_element_type=F32).astype(BF16)
                yield
            c0 += n

    keep = jnp.where((i + seq_tiles - 1) % seq_tiles == 0, 0.0, 1.0)
    _interleave(
        projection(),
        _retention_tile(prev, cos_ref, sin_ref, din_ref, qd_ref, kd_ref, cd_ref, yret_ref, rst, keep, tm),
        _sgu_tile(prev, RET_COLS, sg_lng_ref, sg_lnb_ref, ws_ref, bs_ref, ysg_ref, tm))
    prev[...] = cur[...]


def _inmix(x2, g, w_all, layer, cosf, sinf, din, qd, kd, cd, sg_lng, sg_lnb, ws, bs, tm, S):
    T, D = x2.shape
    nt = T // tm
    seq_tiles = S // tm
    kern = functools.partial(_inmix_kernel, tm=tm, seq_tiles=seq_tiles)
    cur_tile = lambda i: (jnp.minimum(i, nt - 1), 0)
    prev_tile = lambda i: (jnp.maximum(i - 1, 0), 0)
    prev_pos = lambda i: (jnp.maximum(i - 1, 0) % seq_tiles, 0)
    tab = lambda: _const_spec((RET_HEADS, RET_CHUNK, 128))
    return pl.pallas_call(
        kern,
        out_shape=[jax.ShapeDtypeStruct((T, RW_COLS), BF16), jax.ShapeDtypeStruct((T, GATE_COLS), BF16),
                   jax.ShapeDtypeStruct((T, RET_V_WIDTH), BF16), jax.ShapeDtypeStruct((T, SG_WIDTH), BF16)],
        grid=(nt + 1,),
        in_specs=[pl.BlockSpec((tm, D), cur_tile), _const_spec((1, D)),
                  _layer_spec(w_all.shape, layer),
                  pl.BlockSpec((tm, RET_QK_WIDTH), prev_pos), pl.BlockSpec((tm, RET_QK_WIDTH), prev_pos),
                  tab(), tab(), tab(), tab(),
                  _const_spec((1, SG_WIDTH)), _const_spec((1, SG_WIDTH)),
                  _const_spec((SG_GROUPS, SG_CHUNK, SG_CHUNK)),
                  _const_spec((SG_GROUPS, SG_CHUNK, 128))],
        out_specs=[pl.BlockSpec((tm, RW_COLS), cur_tile), pl.BlockSpec((tm, GATE_COLS), cur_tile),
                   pl.BlockSpec((tm, RET_V_WIDTH), prev_tile), pl.BlockSpec((tm, SG_WIDTH), prev_tile)],
        scratch_shapes=[pltpu.VMEM((tm, RET_COLS + SG_COLS), BF16),
                        pltpu.VMEM((tm, RET_COLS + SG_COLS), BF16),
                        pltpu.VMEM((RET_HEADS, 128, RET_V_DIM), F32)],
        compiler_params=_params(("arbitrary",)),
        name="inproj_mix",
    )(x2, g, w_all, cosf, sinf, din, qd, kd, cd, sg_lng, sg_lnb, ws, bs)


def _rwkv_kernel(p_ref, mu_ref, w0_ref, a0_ref, wa_up_ref, g_up_ref, kk_ref, ka_ref, rk_ref,
                 lng_ref, lnb_ref, o_ref, carry, hst, *, ts):
    C = RW_CHUNK
    W = RW_WIDTH
    HD = RW_HEAD_DIM
    HD2 = 2 * HD
    NCH = ts // C
    NP = RW_HEADS // 2
    C2 = 2 * C

    @pl.when(pl.program_id(1) == 0)
    def _():
        carry[...] = jnp.zeros_like(carry)
        hst[...] = jnp.zeros_like(hst)

    p = p_ref[...].astype(F32)
    row8 = lax.broadcasted_iota(jnp.int32, (8, 1), 0)
    rolled = pltpu.roll(p, 1, axis=0)
    p_prev = jnp.concatenate([jnp.where(row8 == 0, carry[7:8, :], rolled[0:8, :]), rolled[8:, :]],
                             axis=0)
    carry[...] = p[ts - 8:ts, :]
    xs = p + mu_ref[...] * (p_prev - p)
    r = xs[:, 0:W]
    k = xs[:, W:2 * W]
    v = xs[:, 2 * W:3 * W]
    wa = xs[:, 3 * W:3 * W + 128]
    gd = xs[:, 3 * W + 128:3 * W + 256]
    lane128 = lax.broadcasted_iota(jnp.int32, (1, 128), 1)
    wa = jnp.where(lane128 < RW_DECAY_RANK, jnp.tanh(wa), wa)
    low = _dot(wa, wa_up_ref[...])
    z = -(w0_ref[...] + low[:, 0:W])
    softplus = jnp.maximum(z, 0.0) + jnp.log(1.0 + jnp.exp(-jnp.abs(z)))
    lw = -jnp.exp(-softplus - 0.5)
    a = _sigmoid(a0_ref[...] + low[:, W:2 * W])
    g = _dot(_sigmoid(gd), g_up_ref[...])

    hi_ = lax.broadcasted_iota(jnp.int32, (256, 256), 0) // HD
    hj_ = lax.broadcasted_iota(jnp.int32, (256, 256), 1) // HD
    ones_blk = jnp.where(hi_ == hj_, 1.0, 0.0).astype(BF16)

    def head_sums(t):
        tb = t.astype(BF16)
        return jnp.concatenate(
            [jnp.dot(tb[:, j:j + 256], ones_blk, preferred_element_type=F32) for j in (0, 256)],
            axis=1)

    kk = k * kk_ref[...]
    kk = kk * lax.rsqrt(jnp.maximum(head_sums(kk * kk), 1e-24))
    k2 = k * (1.0 + (a - 1.0) * ka_ref[...])
    al = -kk
    be = kk * a

    ti =lax.broadcasted_iota(jnp.int32, (C, C), 0)
    tj = lax.broadcasted_iota(jnp.int32, (C, C), 1)
    tri_incl = jnp.where(ti >= tj, 1.0, 0.0).astype(BF16)
    si = lax.broadcasted_iota(jnp.int32, (C2, C2), 0)
    sj = lax.broadcasted_iota(jnp.int32, (C2, C2), 1)
    same_rows = (si // C) == (sj // C)
    lower_strict = jnp.logical_and(same_rows, (si % C) > (sj % C))
    lower_incl = jnp.logical_and(same_rows, (si % C) >= (sj % C))
    vi = lax.broadcasted_iota(jnp.int32, (HD2, HD2), 0) // HD
    vj = lax.broadcasted_iota(jnp.int32, (HD2, HD2), 1) // HD
    same_head = vi == vj
    first_half = lane128 < HD
    second_half = jnp.logical_not(first_half)
    pairs = range(NP)
    cols = [slice(pr * HD2, (pr + 1) * HD2) for pr in pairs]

    def stack_own(t):
        return jnp.concatenate([jnp.where(first_half, t, 0.0), jnp.where(second_half, t, 0.0)], axis=0)

    def stack_other(t):
        return jnp.concatenate([jnp.where(second_half, t, 0.0), jnp.where(first_half, t, 0.0)], axis=0)

    def dup(t):
        return jnp.concatenate([t, t], axis=0)

    def chunk_operands(j):
        sl = slice(j * C, (j + 1) * C)
        lwc = lw[sl, :]
        logp = _dot_exact_lhs(tri_incl, lwc)
        logp_last = logp[C - 1:C, :]
        e_p = jnp.exp(logp)
        e_n = jnp.exp(-logp)
        e_x = jnp.exp(logp - lwc)
        e_l = jnp.exp(logp_last - logp)
        return dict(rb=r[sl, :] * e_p, ab=al[sl, :] * e_x, bt=be[sl, :] * e_n, kt=k2[sl, :] * e_n,
                    btp=be[sl, :] * e_l, ktp=k2[sl, :] * e_l, v=v[sl, :], pend=jnp.exp(logp_last))

    ops = [chunk_operands(j) for j in range(NCH)]
    units = [(j, i) for j in range(NCH) for i in pairs]
    n = range(len(units))
    get = lambda name, u: ops[units[u][0]][name][:, cols[units[u][1]]]
    smat = [_dot_nt(jnp.concatenate([get("ab", u), get("rb", u)], axis=0),
                    jnp.concatenate([stack_own(get("bt", u)).astype(BF16),
                                     stack_own(get("kt", u)).astype(BF16)], axis=0))
            for u in n]
    s_ab = [jnp.where(lower_strict, dup(m[0:C, 0:C2]), 0.0) for m in smat]
    s_ak = [jnp.where(lower_strict, dup(m[0:C, C2:2 * C2]), 0.0) for m in smat]
    s_rbk = [jnp.concatenate([jnp.where(lower_incl, dup(m[C:C2, 0:C2]), 0.0),
                              jnp.where(lower_incl, dup(m[C:C2, C2:2 * C2]), 0.0)],
                             axis=1).astype(BF16) for m in smat]
    v_o = [stack_other(pltpu.roll(get("v", u), HD, axis=1)).astype(BF16) for u in n]
    y = [stack_own(get("ab", u)) + _dot(s_ak[u], v_o[u]) for u in n]
    pm = s_ab
    for _ in range(5):
        z_ = [_dot(pm[u], jnp.concatenate([pm[u], y[u]], axis=1)) for u in n]
        pm = [t[:, 0:C2] for t in z_]
        y = [y[u] + z_[u][:, C2:2 * C2] for u in n]
    y = [y[u] + _dot(pm[u], y[u]) for u in n]
    yv = [jnp.concatenate([y[u].astype(BF16), v_o[u]], axis=0) for u in n]
    ro = [stack_own(get("rb", u)) + jnp.dot(s_rbk[u], yv[u], preferred_element_type=F32)
          for u in n]
    tn = [_dot_tn(yv[u], jnp.concatenate([stack_own(get("btp", u)).astype(BF16),
                                          stack_own(get("ktp", u)).astype(BF16)], axis=0))
          for u in n]

    hs = [hst[i] for i in pairs]
    y_rows = []
    for j in range(NCH):
        y_cols = []
        for i in pairs:
            u = j * NP + i
            hb = hs[i].astype(BF16)
            rp = jnp.where(first_half, ro[u][0:C, :], ro[u][C:C2, :])
            op = pltpu.roll(jnp.where(first_half, ro[u][C:C2, :], ro[u][0:C, :]), HD, axis=1)
            y_cols.append(_dot_nt(rp, hb) + op)
            gt = jnp.where(same_head, tn[u], 0.0)
            n_sw = jnp.where(same_head, 0.0, tn[u])
            nt = jnp.concatenate([n_sw[HD:HD2, :], n_sw[0:HD, :]], axis=0)
            hs[i] = hs[i] * ops[j]["pend"][:, cols[i]] + _dot(hb, gt) + nt
        y_rows.append(jnp.concatenate(y_cols, axis=1))
    for i in pairs:
        hst[i] = hs[i]
    yo = jnp.concatenate(y_rows, axis=0)

    inv_n = 1.0 / HD
    d = yo - head_sums(yo) * inv_n
    var = head_sums(d * d) * inv_n
    yn = d * lax.rsqrt(var + RW_GN_EPS) * lng_ref[...] + lnb_ref[...]
    bonus = head_sums(r * k2 * rk_ref[...]) * v
    o_ref[...] = ((yn + bonus) * g).astype(BF16)


def _rwkv(p3, mu, w0, a0, wa_up, g_up, k_k, k_a, r_k, lng, lnb, ts):
    B, S, _ = p3.shape
    kern = functools.partial(_rwkv_kernel, ts=ts)
    vec = lambda n: _const_spec((1, n))
    return pl.pallas_call(
        kern,
        out_shape=jax.ShapeDtypeStruct((B, S, RW_WIDTH), BF16),
        grid=(B, S // ts),
        in_specs=[pl.BlockSpec((None, ts, RW_COLS), lambda b, s: (b, s, 0)),
                  vec(RW_COLS), vec(RW_WIDTH), vec(RW_WIDTH),
                  _const_spec((128, 2 * RW_WIDTH)), _const_spec((RW_GATE_RANK, RW_WIDTH)),
                  vec(RW_WIDTH), vec(RW_WIDTH), vec(RW_WIDTH), vec(RW_WIDTH), vec(RW_WIDTH)],
        out_specs=pl.BlockSpec((None, ts, RW_WIDTH), lambda b, s: (b, s, 0)),
        scratch_shapes=[pltpu.VMEM((8, RW_COLS), F32),
                        pltpu.VMEM((RW_HEADS // 2, 128, 128), F32)],
        compiler_params=_params(("parallel", "arbitrary")),
        name="rwkv7",
    )(p3, mu, w0, a0, wa_up, g_up, k_k, k_a, r_k, lng, lnb)


def _merge_kernel(x_ref, yr_ref, yt_ref, ys_ref, gl_ref, wb_ref, wo_ref, o_ref):
    acc = None
    for b, y_ref in enumerate((yr_ref, yt_ref, ys_ref)):
        proj = jnp.dot(y_ref[...], wb_ref[b], preferred_element_type=F32)
        gate = _sigmoid(gl_ref[:, b * D_MODEL:(b + 1) * D_MODEL].astype(F32))
        acc = gate * proj if acc is None else acc + gate * proj
    o_ref[...] = x_ref[...] + _dot(acc, wo_ref[...])


def _merge(x2, yr, yt, ys, gl, wb_all, wo_all, layer, tm):
    T, D = x2.shape
    row = lambda n: pl.BlockSpec((tm, n), lambda i: (i, 0))
    return pl.pallas_call(
        _merge_kernel,
        out_shape=jax.ShapeDtypeStruct((T, D), F32),
        grid=(T // tm,),
        in_specs=[row(D), row(512), row(512), row(512), row(GATE_COLS),
                  _layer_spec(wb_all.shape, layer), _layer_spec(wo_all.shape, layer)],
        out_specs=row(D),
        compiler_params=_params(("parallel",)),
        name="merge",
    )(x2, yr, yt, ys, gl, wb_all, wo_all)


def _ffn_kernel(x_ref, g_ref, wu_ref, cw_ref, cb_ref, wd_ref, fg_ref, o_ref, carry, ub, *, ts, final):
    FC = V7X_MXU_DIM
    NFC = D_FF // FC

    @pl.when(pl.program_id(1) == 0)
    def _():
        carry[...] = jnp.zeros_like(carry)

    x = x_ref[...]
    h = _rmsnorm(x, g_ref[...]).astype(BF16)
    col = lambda j, half: slice(half * D_FF + j * FC, half * D_FF + (j + 1) * FC)

    def up(j):
        return [jnp.dot(h, wu_ref[:, col(j, half)], preferred_element_type=F32) for half in range(2)]

    def stage(j, u):
        for half in range(2):
            cs = col(j, half)
            buf = ub.at[2 * (j % 2) + half]
            buf[0:8, :] = carry[:, cs]
            buf[8:8 + ts, :] = u[half]
            carry[:, cs] = u[half][ts - 8:ts, :]

    def conv(j, half):
        cs = col(j, half)
        buf = ub.at[2 * (j % 2) + half]
        return (cb_ref[:, cs] + cw_ref[0:1, cs] * buf[6:6 + ts, :] + cw_ref[1:2, cs] * buf[7:7 + ts, :]
                + cw_ref[2:3, cs] * buf[8:8 + ts, :])

    stage(0, up(0))
    y = x
    act = None
    for j in range(NFC):
        u_next = up(j + 1) if j + 1 < NFC else None
        if act is not None:
            y = y + jnp.dot(act, wd_ref[(j - 1) * FC:j * FC, :], preferred_element_type=F32)
        cg = conv(j, 0)
        cv = conv(j, 1)
        act = (cg * _sigmoid(cg) * cv).astype(BF16)
        if u_next is not None:
            stage(j + 1, u_next)
    y = y + jnp.dot(act, wd_ref[(NFC - 1) * FC:NFC * FC, :], preferred_element_type=F32)
    if final:
        y = _rmsnorm(y, fg_ref[...])
    o_ref[...] = y


def _ffn(x3, g, wu_all, cw, cb, wd_all, layer, fg, ts, final):
    B, S, D = x3.shape
    kern = functools.partial(_ffn_kernel, ts=ts, final=final)
    return pl.pallas_call(
        kern,
        out_shape=jax.ShapeDtypeStruct((B, S, D), F32),
        grid=(B, S // ts),
        in_specs=[pl.BlockSpec((None, ts, D), lambda b, s: (b, s, 0)),
                  _const_spec((1, D)), _layer_spec(wu_all.shape, layer),
                  _const_spec((CONV_WIDTH, 2 * D_FF)), _const_spec((1, 2 * D_FF)),
                  _layer_spec(wd_all.shape, layer), _const_spec((1, D))],
        out_specs=pl.BlockSpec((None, ts, D), lambda b, s: (b, s, 0)),
        scratch_shapes=[pltpu.VMEM((8, 2 * D_FF), F32),
                        pltpu.VMEM((4, ts + 8, V7X_MXU_DIM), F32)],
        compiler_params=_params(("parallel", "arbitrary")),
        name="conv_ffn",
    )(x3, g, wu_all, cw, cb, wd_all, fg)


def _tiles(S):
    pick = lambda pref: min(pref, S)
    return dict(inmix_tm=pick(512), rwkv_ts=pick(RW_TILE_CHUNKS * RW_CHUNK), merge_tm=pick(1024),
                ffn_ts=pick(256))


def _retention_tables(S):
    inv_freq = 1.0 / (ROPE_BASE ** jnp.linspace(0.0, 1.0, RET_QK_DIM // 2, dtype=F32))
    ang = jnp.arange(S, dtype=F32)[:, None] * inv_freq[None, :]
    cos, sin = jnp.cos(ang), jnp.sin(ang)
    cosf = jnp.tile(cos, (1, 2 * RET_HEADS))
    sinf = jnp.tile(jnp.concatenate([-sin, sin], axis=1), (1, RET_HEADS))
    log_gamma = jnp.log(1.0 - 2.0 ** (-5.0 - jnp.arange(RET_HEADS, dtype=F32)))
    pos = jnp.arange(RET_CHUNK, dtype=F32)
    diff = pos[:, None] - pos[None, :]
    din = jnp.where(diff[None] >= 0,
                    jnp.exp(jnp.maximum(diff, 0.0)[None] * log_gamma[:, None, None]), 0.0)
    k_dec = jnp.exp((RET_CHUNK - 1.0 - pos)[None] * log_gamma[:, None])
    q_dec = jnp.exp((pos + 1.0)[None] * log_gamma[:, None])
    c_dec = jnp.exp(RET_CHUNK * log_gamma)
    bc = lambda t: jnp.broadcast_to(t[:, :, None], (RET_HEADS, RET_CHUNK, 128))
    cd = jnp.broadcast_to(c_dec[:, None, None], (RET_HEADS, RET_CHUNK, 128))
    return cosf, sinf, din, bc(q_dec), bc(k_dec), cd


def kernel(x, norm1_g, w_in, rw_mu, rw_w0, rw_w_up, rw_a0, rw_a_up, rw_g_up, rw_k_k, rw_k_a, rw_r_k,
           rw_lnx_g, rw_lnx_b, sg_ln_g, sg_ln_b, sg_w_s, sg_b, w_branch, w_out, norm2_g, ffn_w_up,
           ffn_conv_w, ffn_conv_b, ffn_w_down, final_g):
    B, S, D = x.shape
    T = B * S
    depth = w_in.shape[0]
    t = _tiles(S)
    cosf, sinf, din, qd, kd, cd = _retention_tables(S)
    row = lambda a: a.reshape(1, -1)
    zeros = jnp.zeros((RW_DECAY_RANK, RW_WIDTH), F32)
    w_in_b, w_branch_b, w_out_b = w_in.astype(BF16), w_branch.astype(BF16), w_out.astype(BF16)
    ffn_w_up_b, ffn_w_down_b = ffn_w_up.astype(BF16), ffn_w_down.astype(BF16)
    for l in range(depth):
        x2 = x.reshape(T, D)
        bs = jnp.broadcast_to(sg_b[l][:, :, None], (SG_GROUPS, SG_CHUNK, 128))
        p_rw, gl, y_ret, y_sg = _inmix(x2, row(norm1_g[l]), w_in_b, l, cosf, sinf, din, qd, kd, cd,
                                       row(sg_ln_g[l]), row(sg_ln_b[l]), sg_w_s[l], bs,
                                       t["inmix_tm"], S)
        wa_up = jnp.concatenate(
            [jnp.concatenate([rw_w_up[l], zeros], axis=1),
             jnp.concatenate([zeros, rw_a_up[l]], axis=1)], axis=0).astype(BF16)
        y_rw = _rwkv(p_rw.reshape(B, S, -1), row(rw_mu[l]), row(rw_w0[l]), row(rw_a0[l]), wa_up,
                     rw_g_up[l].astype(BF16), row(rw_k_k[l]), row(rw_k_a[l]), row(rw_r_k[l]),
                     row(rw_lnx_g[l]), row(rw_lnx_b[l]), t["rwkv_ts"])
        x2 = _merge(x2, y_rw.reshape(T, -1), y_ret, y_sg, gl, w_branch_b, w_out_b, l, t["merge_tm"])
        x = _ffn(x2.reshape(B, S, D), row(norm2_g[l]), ffn_w_up_b, ffn_conv_w[l], row(ffn_conv_b[l]),
                 ffn_w_down_b, l, row(final_g), t["ffn_ts"], final=(l == depth - 1))
    return x
```

```python
import functools

import jax
import jax.numpy as jnp
from jax import lax
from jax.experimental import pallas as pl
from jax.experimental.pallas import tpu as pltpu

F32 = jnp.float32
BF16 = jnp.bfloat16

D_MODEL = 1024
RW_HEADS = 8
RW_HEAD_DIM = 64
RW_WIDTH = 512
RW_DECAY_RANK = 64
RW_ICLR_RANK = 64
RW_GATE_RANK = 128
RW_COLS = 1792
RW_GN_EPS = 64e-5
RW_CHUNK = 64
RW_TILE_CHUNKS = 4
RET_HEADS = 4
RET_QK_DIM = 64
RET_V_DIM = 128
RET_QK_WIDTH = 256
RET_V_WIDTH = 512
RET_COLS = 1536
RET_CHUNK = 128
ROPE_BASE = 10000.0
SG_GROUPS = 4
SG_CHUNK = 128
SG_WIDTH = 512
SG_COLS = 1024
N_BRANCH = 3
GATE_COLS = 3072
MIX_COLS = RW_COLS + RET_COLS + SG_COLS
D_FF = 2816
CONV_WIDTH = 3
EPS = 1e-6

V7X_MXU_DIM = 256
V7X_VMEM_LIMIT_BYTES = 60 * 1024 * 1024


def _dot(a, b):
    return jnp.dot(a.astype(BF16), b.astype(BF16), preferred_element_type=F32)


def _dot_nt(a, b):
    return lax.dot_general(a.astype(BF16), b.astype(BF16), (((1,), (1,)), ((), ())),
                           preferred_element_type=F32)


def _dot_tn(a, b):
    return lax.dot_general(a.astype(BF16), b.astype(BF16), (((0,), (0,)), ((), ())),
                           preferred_element_type=F32)


def _split(a):
    hi = a.astype(BF16)
    lo = (a - hi.astype(F32)).astype(BF16)
    return hi, lo


def _dot_exact_lhs(a_bf16, b):
    hi, lo = _split(b)
    return (jnp.dot(a_bf16, hi, preferred_element_type=F32)
            + jnp.dot(a_bf16, lo, preferred_element_type=F32))


def _sigmoid(x):
    return 0.5 + 0.5 * jnp.tanh(0.5 * x)


def _rmsnorm(x, g):
    return x * lax.rsqrt(jnp.mean(x * x, axis=-1, keepdims=True) + EPS) * g


def _params(sem):
    return pltpu.CompilerParams(dimension_semantics=sem, vmem_limit_bytes=V7X_VMEM_LIMIT_BYTES)


def _const_spec(shape):
    nd = len(shape)
    return pl.BlockSpec(shape, lambda *_: (0,) * nd, pipeline_mode=pl.Buffered(1))


def _layer_spec(stacked_shape, layer):
    nd = len(stacked_shape) - 1
    return pl.BlockSpec((None,) + tuple(stacked_shape[1:]), lambda *_: (layer,) + (0,) * nd,
                        pipeline_mode=pl.Buffered(1))


def _interleave(main, n_main, others, others_cost):
    live = list(others)
    spent = 0
    for m in range(n_main):
        next(main)
        while live and spent * n_main < (m + 1) * others_cost:
            g = live.pop(0)
            try:
                spent += next(g)
                live.append(g)
            except StopIteration:
                pass
    for g in [main] + live:
        for _ in g:
            pass


_COST_PREP = (1800, 1600, 1300)
_COST_RET = (400, 330)
_COST_SGU = (1400, 150)


def _retention_tile(p_ref, cos_ref, sin_ref, din_ref, qd_ref, kd_ref, cd_ref, o_ref, rst, keep, tm):
    C = RET_CHUNK
    lane = lax.broadcasted_iota(jnp.int32, (1, RET_QK_WIDTH), 1)
    first = (lane % RET_QK_DIM) < (RET_QK_DIM // 2)
    lane128 = lax.broadcasted_iota(jnp.int32, (1, 128), 1)
    first_head = lane128 < RET_QK_DIM

    def rot(t, cs, sn):
        sw = jnp.where(first, pltpu.roll(t, RET_QK_WIDTH - RET_QK_DIM // 2, axis=1),
                       pltpu.roll(t, RET_QK_DIM // 2, axis=1))
        return t * cs + sw * sn

    v0 = 2 * RET_QK_WIDTH
    g0 = v0 + RET_V_WIDTH
    state = [rst[h] * keep for h in range(RET_HEADS)]
    for c in range(tm // C):
        rows = slice(c * C, (c + 1) * C)
        cs = cos_ref[rows, :]
        sn = sin_ref[rows, :]
        q = rot(p_ref[rows, 0:RET_QK_WIDTH].astype(F32), cs, sn)
        k = rot(p_ref[rows, RET_QK_WIDTH:2 * RET_QK_WIDTH].astype(F32), cs, sn) * (RET_QK_DIM ** -0.5)
        yield _COST_RET[0]
        for h in range(RET_HEADS):
            pr = h // 2
            msk = first_head if h % 2 == 0 else jnp.logical_not(first_head)
            qm = jnp.where(msk, q[:, pr * 128:(pr + 1) * 128], 0.0).astype(BF16)
            kp = k[:, pr * 128:(pr + 1) * 128]
            vh = p_ref[rows, v0 + h * RET_V_DIM:v0 + (h + 1) * RET_V_DIM]
            gh = p_ref[rows, g0 + h * RET_V_DIM:g0 + (h + 1) * RET_V_DIM].astype(F32)
            scores = _dot_nt(qm, kp) * din_ref[h]
            y = _dot(scores, vh) + _dot(qm, state[h]) * qd_ref[h]
            km = jnp.where(msk, kp, 0.0) * kd_ref[h]
            state[h] = state[h] * cd_ref[h] + _dot_tn(km, vh)
            y = y * lax.rsqrt(jnp.mean(y * y, axis=-1, keepdims=True) + EPS)
            o_ref[rows, h * RET_V_DIM:(h + 1) * RET_V_DIM] = (y * (gh * _sigmoid(gh))).astype(BF16)
            yield _COST_RET[1]
    for h in range(RET_HEADS):
        rst[h] = state[h]


def _sgu_tile(p_ref, c0, lng_ref, lnb_ref, ws_ref, bs_ref, o_ref, tm):
    C = SG_CHUNK
    ti = lax.broadcasted_iota(jnp.int32, (C, C), 0)
    tj = lax.broadcasted_iota(jnp.int32, (C, C), 1)
    causal = ti >= tj
    w = [jnp.where(causal, ws_ref[g], 0.0).astype(BF16) for g in range(SG_GROUPS)]
    for c in range(tm // C):
        rows = slice(c * C, (c + 1) * C)
        p = p_ref[rows, c0:c0 + SG_COLS].astype(F32)
        z = 0.5 * p * (1.0 + lax.erf(p * (2.0 ** -0.5)))
        u = z[:, 0:SG_WIDTH]
        vf = z[:, SG_WIDTH:2 * SG_WIDTH]
        mean = jnp.mean(vf, axis=-1, keepdims=True)
        d = vf - mean
        var = jnp.mean(d * d, axis=-1, keepdims=True)
        vn = d * lax.rsqrt(var + EPS) * lng_ref[...] + lnb_ref[...]
        yield _COST_SGU[0]
        for g in range(SG_GROUPS):
            gc = slice(g * 128, (g + 1) * 128)
            mixed = _dot(w[g], vn[:, gc]) + bs_ref[g]
            o_ref[rows, gc] = (u[:, gc] * mixed).astype(BF16)
        yield _COST_SGU[1]


def _rwkv_prep_tile(p_ref, mu_ref, w0_ref, a0_ref, wa_up_ref, g_up_ref, kk_ref, ka_ref, ones_blk,
                    outs, carry, keep, tm):
    W = RW_WIDTH
    RB = 128
    r_ref, k_ref, v_ref, al_ref, be_ref, lw_ref, g_ref = outs
    row8 = lax.broadcasted_iota(jnp.int32, (8, 1), 0)
    lane128 = lax.broadcasted_iota(jnp.int32, (1, 128), 1)
    last8 = carry[...] * keep
    for b in range(tm // RB):
        rows = slice(b * RB, (b + 1) * RB)
        p = p_ref[rows, 0:RW_COLS].astype(F32)
        rolled = pltpu.roll(p, 1, axis=0)
        p_prev = jnp.concatenate([jnp.where(row8 == 0, last8[7:8, :], rolled[0:8, :]), rolled[8:, :]],
                                 axis=0)
        last8 = p[RB - 8:RB, :]
        xs = p + mu_ref[...] * (p_prev - p)
        r = xs[:, 0:W]
        k = xs[:, W:2 * W]
        wa = xs[:, 3 * W:3 * W + 128]
        gd = xs[:, 3 * W + 128:3 * W + 256]
        r_ref[rows, :] = r.astype(BF16)
        v_ref[rows, :] = xs[:, 2 * W:3 * W].astype(BF16)
        yield _COST_PREP[0]
        wa = jnp.where(lane128 < RW_DECAY_RANK, jnp.tanh(wa), wa)
        low = _dot(wa, wa_up_ref[...])
        z = -(w0_ref[...] + low[:, 0:W])
        softplus = jnp.maximum(z, 0.0) + jnp.log(1.0 + jnp.exp(-jnp.abs(z)))
        lw_ref[rows, :] = -jnp.exp(-softplus - 0.5)
        a = _sigmoid(a0_ref[...] + low[:, W:2 * W])
        g_ref[rows, :] = _dot(_sigmoid(gd), g_up_ref[...]).astype(BF16)
        yield _COST_PREP[1]
        kk = k * kk_ref[...]
        sq = (kk * kk).astype(BF16)
        ss = jnp.concatenate([jnp.dot(sq[:, j:j + 256], ones_blk, preferred_element_type=F32)
                              for j in (0, 256)], axis=1)
        kk = kk * lax.rsqrt(jnp.maximum(ss, 1e-24))
        k_ref[rows, :] = (k * (1.0 + (a - 1.0) * ka_ref[...])).astype(BF16)
        al_ref[rows, :] = (-kk).astype(BF16)
        be_ref[rows, :] = (kk * a).astype(BF16)
        yield _COST_PREP[2]
    carry[...] = last8


def _inmix_kernel(x_ref, g_ref, w_ref, cos_ref, sin_ref, din_ref, qd_ref, kd_ref, cd_ref,
                  sg_lng_ref, sg_lnb_ref, ws_ref, bs_ref,
                  mu_ref, w0_ref, a0_ref, wa_up_ref, g_up_ref, kk_ref, ka_ref,
                  yret_ref, ysg_ref, r_ref, k_ref, v_ref, al_ref, be_ref, lw_ref, gate_ref,
                  cur, prev, rst, rw_carry, *, tm, seq_tiles):
    i = pl.program_id(0)

    @pl.when(i == 0)
    def _():
        prev[...] = jnp.zeros_like(prev)
        rst[...] = jnp.zeros_like(rst)
        rw_carry[...] = jnp.zeros_like(rw_carry)

    h = _rmsnorm(x_ref[...], g_ref[...]).astype(BF16)

    def projection():
        MX = V7X_MXU_DIM
        for j in range(0, MIX_COLS, MX):
            cur[:, j:j + MX] = jnp.dot(h, w_ref[:, j:j + MX], preferred_element_type=F32).astype(BF16)
            yield

    keep = jnp.where((i + seq_tiles - 1) % seq_tiles == 0, 0.0, 1.0)
    hi_ = lax.broadcasted_iota(jnp.int32, (256, 256), 0) // RW_HEAD_DIM
    hj_ = lax.broadcasted_iota(jnp.int32, (256, 256), 1) // RW_HEAD_DIM
    ones_blk = jnp.where(hi_ == hj_, 1.0, 0.0).astype(BF16)
    mixers = [
        _rwkv_prep_tile(prev, mu_ref, w0_ref, a0_ref, wa_up_ref, g_up_ref, kk_ref, ka_ref, ones_blk,
                        (r_ref, k_ref, v_ref, al_ref, be_ref, lw_ref, gate_ref), rw_carry, keep, tm),
        _retention_tile(prev.at[:, RW_COLS:RW_COLS + RET_COLS], cos_ref, sin_ref, din_ref, qd_ref,
                        kd_ref, cd_ref, yret_ref, rst, keep, tm),
        _sgu_tile(prev, RW_COLS + RET_COLS, sg_lng_ref, sg_lnb_ref, ws_ref, bs_ref, ysg_ref, tm)]
    mixer_cost = ((tm // 128) * sum(_COST_PREP)
                  + (tm // RET_CHUNK) * (_COST_RET[0] + RET_HEADS * _COST_RET[1])
                  + (tm // SG_CHUNK) * sum(_COST_SGU))
    _interleave(projection(), MIX_COLS // V7X_MXU_DIM, mixers, mixer_cost)
    prev[...] = cur[...]


def _inmix(x2, g, w_all, layer, cosf, sinf, din, qd, kd, cd, sg_lng, sg_lnb, ws, bs,
           mu, w0, a0, wa_up, g_up, k_k, k_a, tm, S):
    T, D = x2.shape
    nt = T // tm
    seq_tiles = S // tm
    kern = functools.partial(_inmix_kernel, tm=tm, seq_tiles=seq_tiles)
    cur_tile = lambda i: (jnp.minimum(i, nt - 1), 0)
    prev_tile = lambda i: (jnp.maximum(i - 1, 0), 0)
    prev_pos = lambda i: (jnp.maximum(i - 1, 0) % seq_tiles, 0)
    tab = lambda: _const_spec((RET_HEADS, RET_CHUNK, 128))
    vec = lambda n: _const_spec((1, n))
    half = lambda dt: jax.ShapeDtypeStruct((T, RW_WIDTH), dt)
    prev_out = lambda n: pl.BlockSpec((tm, n), prev_tile)
    return pl.pallas_call(
        kern,
        out_shape=[half(BF16), half(BF16),
                   half(BF16), half(BF16), half(BF16), half(BF16), half(BF16), half(F32), half(BF16)],
        grid=(nt + 1,),
        in_specs=[pl.BlockSpec((tm, D), cur_tile), vec(D),
                  _layer_spec(w_all.shape, layer),
                  pl.BlockSpec((tm, RET_QK_WIDTH), prev_pos), pl.BlockSpec((tm, RET_QK_WIDTH), prev_pos),
                  tab(), tab(), tab(), tab(),
                  vec(SG_WIDTH), vec(SG_WIDTH),
                  _const_spec((SG_GROUPS, SG_CHUNK, SG_CHUNK)),
                  _const_spec((SG_GROUPS, SG_CHUNK, 128)),
                  vec(RW_COLS), vec(RW_WIDTH), vec(RW_WIDTH),
                  _const_spec((128, 2 * RW_WIDTH)), _const_spec((RW_GATE_RANK, RW_WIDTH)),
                  vec(RW_WIDTH), vec(RW_WIDTH)],
        out_specs=[prev_out(RW_WIDTH)] * 9,
        scratch_shapes=[pltpu.VMEM((tm, MIX_COLS), BF16),
                        pltpu.VMEM((tm, MIX_COLS), BF16),
                        pltpu.VMEM((RET_HEADS, 128, RET_V_DIM), F32),
                        pltpu.VMEM((8, RW_COLS), F32)],
        compiler_params=_params(("arbitrary",)),
        name="inproj_mix",
    )(x2, g, w_all, cosf, sinf, din, qd, kd, cd, sg_lng, sg_lnb, ws, bs, mu, w0, a0, wa_up, g_up,
      k_k, k_a)


def _rwkv_kernel(r_ref, k_ref, v_ref, al_ref, be_ref, lw_ref, g_ref, rk_ref, lng_ref, lnb_ref, o_ref,
                 hst, *, ts):
    C = RW_CHUNK
    W = RW_WIDTH
    HD = RW_HEAD_DIM
    HD2 = 2 * HD
    NCH = ts // C
    NP = RW_HEADS // 2
    C2 = 2 * C

    @pl.when(pl.program_id(1) == 0)
    def _():
        hst[...] = jnp.zeros_like(hst)

    r = r_ref[...].astype(F32)
    k2 = k_ref[...].astype(F32)
    v = v_ref[...].astype(F32)
    al = al_ref[...].astype(F32)
    be = be_ref[...].astype(F32)
    lw = lw_ref[...]
    lane128 = lax.broadcasted_iota(jnp.int32, (1, 128), 1)

    hi_ = lax.broadcasted_iota(jnp.int32, (256, 256), 0) // HD
    hj_ = lax.broadcasted_iota(jnp.int32, (256, 256), 1) // HD
    ones_blk = jnp.where(hi_ == hj_, 1.0, 0.0).astype(BF16)

    def head_sums(t):
        tb = t.astype(BF16)
        return jnp.concatenate(
            [jnp.dot(tb[:, j:j + 256], ones_blk, preferred_element_type=F32) for j in (0, 256)],
            axis=1)

    ti =lax.broadcasted_iota(jnp.int32, (C, C), 0)
    tj = lax.broadcasted_iota(jnp.int32, (C, C), 1)
    tri_incl = jnp.where(ti >= tj, 1.0, 0.0).astype(BF16)
    si = lax.broadcasted_iota(jnp.int32, (C2, C2), 0)
    sj = lax.broadcasted_iota(jnp.int32, (C2, C2), 1)
    same_rows = (si // C) == (sj // C)
    lower_strict = jnp.logical_and(same_rows, (si % C) > (sj % C))
    lower_incl = jnp.logical_and(same_rows, (si % C) >= (sj % C))
    vi = lax.broadcasted_iota(jnp.int32, (HD2, HD2), 0) // HD
    vj = lax.broadcasted_iota(jnp.int32, (HD2, HD2), 1) // HD
    same_head = vi == vj
    first_half = lane128 < HD
    second_half = jnp.logical_not(first_half)
    pairs = range(NP)
    cols = [slice(pr * HD2, (pr + 1) * HD2) for pr in pairs]

    def stack_own(t):
        return jnp.concatenate([jnp.where(first_half, t, 0.0), jnp.where(second_half, t, 0.0)], axis=0)

    def stack_other(t):
        return jnp.concatenate([jnp.where(second_half, t, 0.0), jnp.where(first_half, t, 0.0)], axis=0)

    def dup(t):
        return jnp.concatenate([t, t], axis=0)

    def chunk_operands(j):
        sl = slice(j * C, (j + 1) * C)
        lwc = lw[sl, :]
        logp = _dot_exact_lhs(tri_incl, lwc)
        logp_last = logp[C - 1:C, :]
        e_p = jnp.exp(logp)
        e_n = jnp.exp(-logp)
        e_x = jnp.exp(logp - lwc)
        e_l = jnp.exp(logp_last - logp)
        return dict(rb=r[sl, :] * e_p, ab=al[sl, :] * e_x, bt=be[sl, :] * e_n, kt=k2[sl, :] * e_n,
                    btp=be[sl, :] * e_l, ktp=k2[sl, :] * e_l, v=v[sl, :], pend=jnp.exp(logp_last))

    ops = [chunk_operands(j) for j in range(NCH)]
    units = [(j, i) for j in range(NCH) for i in pairs]
    n = range(len(units))
    get = lambda name, u: ops[units[u][0]][name][:, cols[units[u][1]]]
    smat = [_dot_nt(jnp.concatenate([get("ab", u), get("rb", u)], axis=0),
                    jnp.concatenate([stack_own(get("bt", u)).astype(BF16),
                                     stack_own(get("kt", u)).astype(BF16)], axis=0))
            for u in n]
    s_ab = [jnp.where(lower_strict, dup(m[0:C, 0:C2]), 0.0) for m in smat]
    s_ak = [jnp.where(lower_strict, dup(m[0:C, C2:2 * C2]), 0.0) for m in smat]
    s_rbk = [jnp.concatenate([jnp.where(lower_incl, dup(m[C:C2, 0:C2]), 0.0),
                              jnp.where(lower_incl, dup(m[C:C2, C2:2 * C2]), 0.0)],
                             axis=1).astype(BF16) for m in smat]
    v_o = [stack_other(pltpu.roll(get("v", u), HD, axis=1)).astype(BF16) for u in n]
    y = [stack_own(get("ab", u)) + _dot(s_ak[u], v_o[u]) for u in n]
    pm = s_ab
    for _ in range(4):
        z_ = [_dot(pm[u], jnp.concatenate([pm[u], y[u]], axis=1)) for u in n]
        pm = [t[:, 0:C2] for t in z_]
        y = [y[u] + z_[u][:, C2:2 * C2] for u in n]
    Q, H2 = C // 4, C // 2
    z_ = [_dot(jnp.concatenate([pm[u][Q:C, :], pm[u][C + Q:C2, :]], axis=0),
               jnp.concatenate([pm[u], y[u]], axis=1)) for u in n]
    p5 = [jnp.concatenate([t[H2 - Q:C - Q, 0:C2], t[C - Q + H2 - Q:2 * (C - Q), 0:C2]], axis=0)
          for t in z_]
    y = [jnp.concatenate([y[u][0:Q, :], y[u][Q:C, :] + z_[u][0:C - Q, C2:2 * C2],
                          y[u][C:C + Q, :], y[u][C + Q:C2, :] + z_[u][C - Q:2 * (C - Q), C2:2 * C2]],
                         axis=0) for u in n]
    z_ = [_dot(p5[u], y[u]) for u in n]
    y = [jnp.concatenate([y[u][0:H2, :], y[u][H2:C, :] + z_[u][0:H2, :],
                          y[u][C:C + H2, :], y[u][C + H2:C2, :] + z_[u][H2:C, :]], axis=0) for u in n]
    yv = [jnp.concatenate([y[u].astype(BF16), v_o[u]], axis=0) for u in n]
    ro = [stack_own(get("rb", u)) + jnp.dot(s_rbk[u], yv[u], preferred_element_type=F32)
          for u in n]
    tn = [_dot_tn(yv[u], jnp.concatenate([stack_own(get("btp", u)).astype(BF16),
                                          stack_own(get("ktp", u)).astype(BF16)], axis=0))
          for u in n]

    hs = [hst[i] for i in pairs]
    y_rows = []
    for j in range(NCH):
        y_cols = []
        for i in pairs:
            u = j * NP + i
            hb = hs[i].astype(BF16)
            rp = jnp.where(first_half, ro[u][0:C, :], ro[u][C:C2, :])
            op = pltpu.roll(jnp.where(first_half, ro[u][C:C2, :], ro[u][0:C, :]), HD, axis=1)
            y_cols.append(_dot_nt(rp, hb) + op)
            gt = jnp.where(same_head, tn[u], 0.0)
            n_sw = jnp.where(same_head, 0.0, tn[u])
            nt = jnp.concatenate([n_sw[HD:HD2, :], n_sw[0:HD, :]], axis=0)
            hs[i] = hs[i] * ops[j]["pend"][:, cols[i]] + _dot(hb, gt) + nt
        y_rows.append(jnp.concatenate(y_cols, axis=1))
    for i in pairs:
        hst[i] = hs[i]
    yo = jnp.concatenate(y_rows, axis=0)

    inv_n = 1.0 / HD
    d = yo - head_sums(yo) * inv_n
    var = head_sums(d * d) * inv_n
    yn = d * lax.rsqrt(var + RW_GN_EPS) * lng_ref[...] + lnb_ref[...]
    bonus = head_sums(r * k2 * rk_ref[...]) * v
    o_ref[...] = ((yn + bonus) * g_ref[...].astype(F32)).astype(BF16)


def _rwkv(r, k2, v, al, be, lw, g, r_k, lng, lnb, B, ts):
    T = r.shape[0]
    S = T // B
    kern = functools.partial(_rwkv_kernel, ts=ts)
    vec = lambda n: _const_spec((1, n))
    tile = lambda: pl.BlockSpec((None, ts, RW_WIDTH), lambda b, s: (b, s, 0))
    seq = lambda a: a.reshape(B, S, RW_WIDTH)
    return pl.pallas_call(
        kern,
        out_shape=jax.ShapeDtypeStruct((B, S, RW_WIDTH), BF16),
        grid=(B, S // ts),
        in_specs=[tile(), tile(), tile(), tile(), tile(), tile(), tile(),
                  vec(RW_WIDTH), vec(RW_WIDTH), vec(RW_WIDTH)],
        out_specs=tile(),
        scratch_shapes=[pltpu.VMEM((RW_HEADS // 2, 128, 128), F32)],
        compiler_params=_params(("parallel", "arbitrary")),
        name="rwkv7",
    )(seq(r), seq(k2), seq(v), seq(al), seq(be), seq(lw), seq(g), r_k, lng, lnb)


def _merge_kernel(x_ref, g_ref, yr_ref, yt_ref, ys_ref, wgl_ref, wb_ref, wo_ref, o_ref):
    x = x_ref[...]
    h = _rmsnorm(x, g_ref[...]).astype(BF16)
    NC = 512
    parts = []
    for c in range(0, D_MODEL, NC):
        acc = None
        for b, y_ref in enumerate((yr_ref, yt_ref, ys_ref)):
            proj = jnp.dot(y_ref[...], wb_ref[b, :, c:c + NC], preferred_element_type=F32)
            logits = jnp.dot(h, wgl_ref[:, b * D_MODEL + c:b * D_MODEL + c + NC],
                             preferred_element_type=F32)
            acc = _sigmoid(logits) * proj if acc is None else acc + _sigmoid(logits) * proj
        parts.append(acc.astype(BF16))
    o_ref[...] = x + jnp.dot(jnp.concatenate(parts, axis=1), wo_ref[...], preferred_element_type=F32)


def _merge(x2, g, yr, yt, ys, wgl_all, wb_all, wo_all, layer, tm):
    T, D = x2.shape
    row = lambda n: pl.BlockSpec((tm, n), lambda i: (i, 0))
    return pl.pallas_call(
        _merge_kernel,
        out_shape=jax.ShapeDtypeStruct((T, D), F32),
        grid=(T // tm,),
        in_specs=[row(D), _const_spec((1, D)), row(512), row(512), row(512),
                  _layer_spec(wgl_all.shape, layer), _layer_spec(wb_all.shape, layer),
                  _layer_spec(wo_all.shape, layer)],
        out_specs=row(D),
        compiler_params=_params(("parallel",)),
        name="merge",
    )(x2, g, yr, yt, ys, wgl_all, wb_all, wo_all)


def _ffn_kernel(x_ref, g_ref, wu_ref, cw_ref, cb_ref, wd_ref, fg_ref, o_ref, carry, ub, act_s,
                *, ts, final):
    FC = V7X_MXU_DIM
    NFC = D_FF // FC

    @pl.when(pl.program_id(1) == 0)
    def _():
        carry[...] = jnp.zeros_like(carry)

    x = x_ref[...]
    h = _rmsnorm(x, g_ref[...]).astype(BF16)
    col = lambda j, half: slice(half * D_FF + j * FC, half * D_FF + (j + 1) * FC)

    def up(j):
        for half in range(2):
            cs = col(j, half)
            buf = ub.at[2 * (j % 2) + half]
            buf[0:8, :] = carry[:, cs]
            buf[8:8 + ts, :] = jnp.dot(h, wu_ref[:, cs], preferred_element_type=F32)
            carry[:, cs] = buf[ts:ts + 8, :]

    def conv(j, half):
        cs = col(j, half)
        buf = ub.at[2 * (j % 2) + half]
        return (cb_ref[:, cs] + cw_ref[0:1, cs] * buf[6:6 + ts, :] + cw_ref[1:2, cs] * buf[7:7 + ts, :]
                + cw_ref[2:3, cs] * buf[8:8 + ts, :])

    up(0)
    for j in range(NFC):
        if j + 1 < NFC:
            up(j + 1)
        cg = conv(j, 0)
        cv = conv(j, 1)
        act_s[:, j * FC:(j + 1) * FC] = (cg * _sigmoid(cg) * cv).astype(BF16)
    y = x + jnp.dot(act_s[...], wd_ref[...], preferred_element_type=F32)
    if final:
        y = _rmsnorm(y, fg_ref[...])
    o_ref[...] = y


def _ffn(x3, g, wu_all, cw, cb, wd_all, layer, fg, ts, final):
    B, S, D = x3.shape
    kern = functools.partial(_ffn_kernel, ts=ts, final=final)
    return pl.pallas_call(
        kern,
        out_shape=jax.ShapeDtypeStruct((B, S, D), F32),
        grid=(B, S // ts),
        in_specs=[pl.BlockSpec((None, ts, D), lambda b, s: (b, s, 0)),
                  _const_spec((1, D)), _layer_spec(wu_all.shape, layer),
                  _const_spec((CONV_WIDTH, 2 * D_FF)), _const_spec((1, 2 * D_FF)),
                  _layer_spec(wd_all.shape, layer), _const_spec((1, D))],
        out_specs=pl.BlockSpec((None, ts, D), lambda b, s: (b, s, 0)),
        scratch_shapes=[pltpu.VMEM((8, 2 * D_FF), F32),
                        pltpu.VMEM((4, ts + 8, V7X_MXU_DIM), F32),
                        pltpu.VMEM((ts, D_FF), BF16)],
        compiler_params=_params(("parallel", "arbitrary")),
        name="conv_ffn",
    )(x3, g, wu_all, cw, cb, wd_all, fg)


def _tiles(S):
    pick = lambda pref: min(pref, S)
    return dict(inmix_tm=pick(512), rwkv_ts=pick(RW_TILE_CHUNKS * RW_CHUNK), merge_tm=pick(512),
                ffn_ts=pick(256))


def _retention_tables(S):
    inv_freq = 1.0 / (ROPE_BASE ** jnp.linspace(0.0, 1.0, RET_QK_DIM // 2, dtype=F32))
    ang = jnp.arange(S, dtype=F32)[:, None] * inv_freq[None, :]
    cos, sin = jnp.cos(ang), jnp.sin(ang)
    cosf = jnp.tile(cos, (1, 2 * RET_HEADS))
    sinf = jnp.tile(jnp.concatenate([-sin, sin], axis=1), (1, RET_HEADS))
    log_gamma = jnp.log(1.0 - 2.0 ** (-5.0 - jnp.arange(RET_HEADS, dtype=F32)))
    pos = jnp.arange(RET_CHUNK, dtype=F32)
    diff = pos[:, None] - pos[None, :]
    din = jnp.where(diff[None] >= 0,
                    jnp.exp(jnp.maximum(diff, 0.0)[None] * log_gamma[:, None, None]), 0.0)
    k_dec = jnp.exp((RET_CHUNK - 1.0 - pos)[None] * log_gamma[:, None])
    q_dec = jnp.exp((pos + 1.0)[None] * log_gamma[:, None])
    c_dec = jnp.exp(RET_CHUNK * log_gamma)
    bc = lambda t: jnp.broadcast_to(t[:, :, None], (RET_HEADS, RET_CHUNK, 128))
    cd = jnp.broadcast_to(c_dec[:, None, None], (RET_HEADS, RET_CHUNK, 128))
    return cosf, sinf, din, bc(q_dec), bc(k_dec), cd


def kernel(x, norm1_g, w_in, rw_mu, rw_w0, rw_w_up, rw_a0, rw_a_up, rw_g_up, rw_k_k, rw_k_a, rw_r_k,
           rw_lnx_g, rw_lnx_b, sg_ln_g, sg_ln_b, sg_w_s, sg_b, w_branch, w_out, norm2_g, ffn_w_up,
           ffn_conv_w, ffn_conv_b, ffn_w_down, final_g):
    B, S, D = x.shape
    T = B * S
    depth = w_in.shape[0]
    t = _tiles(S)
    cosf, sinf, din, qd, kd, cd = _retention_tables(S)
    row = lambda a: a.reshape(1, -1)
    zeros = jnp.zeros((RW_DECAY_RANK, RW_WIDTH), F32)
    w_mix_b, w_gl_b = w_in[:, :, :MIX_COLS].astype(BF16), w_in[:, :, MIX_COLS:].astype(BF16)
    w_branch_b, w_out_b = w_branch.astype(BF16), w_out.astype(BF16)
    ffn_w_up_b, ffn_w_down_b = ffn_w_up.astype(BF16), ffn_w_down.astype(BF16)
    for l in range(depth):
        x2 = x.reshape(T, D)
        bs = jnp.broadcast_to(sg_b[l][:, :, None], (SG_GROUPS, SG_CHUNK, 128))
        wa_up = jnp.concatenate(
            [jnp.concatenate([rw_w_up[l], zeros], axis=1),
             jnp.concatenate([zeros, rw_a_up[l]], axis=1)], axis=0).astype(BF16)
        y_ret, y_sg, r, k2, v, al, be, lw, g = _inmix(
            x2, row(norm1_g[l]), w_mix_b, l, cosf, sinf, din, qd, kd, cd,
            row(sg_ln_g[l]), row(sg_ln_b[l]), sg_w_s[l], bs,
            row(rw_mu[l]), row(rw_w0[l]), row(rw_a0[l]), wa_up, rw_g_up[l].astype(BF16),
            row(rw_k_k[l]), row(rw_k_a[l]), t["inmix_tm"], S)
        y_rw = _rwkv(r, k2, v, al, be, lw, g, row(rw_r_k[l]), row(rw_lnx_g[l]), row(rw_lnx_b[l]),
                     B, t["rwkv_ts"])
        x2 = _merge(x2, row(norm1_g[l]), y_rw.reshape(T, -1), y_ret, y_sg, w_gl_b, w_branch_b, w_out_b, l,
                    t["merge_tm"])
        x = _ffn(x2.reshape(B, S, D), row(norm2_g[l]), ffn_w_up_b, ffn_conv_w[l], row(ffn_conv_b[l]),
                 ffn_w_down_b, l, row(final_g), t["ffn_ts"], final=(l == depth - 1))
    return x
```
